```python
import math
import jax, jax.numpy as jnp
from jax import lax
import numpy as np

D_MODEL = 4096
BATCH = 4
SEQ = 2048
DEPTH = 2
DEC_BATCH = 8
DEC_SEQ = 1
PAST_LEN = 16384
PAGE_SIZE = 128

HEAD_DIM = 128
N_HEADS = D_MODEL // HEAD_DIM
H_RET = N_HEADS // 2
H_ATT = N_HEADS - H_RET
W_RET = H_RET * HEAD_DIM
W_ATT = H_ATT * HEAD_DIM
MIX_WIDTH = W_RET + W_ATT
IN_COLS = 4 * W_RET + 3 * W_ATT
D_FF = (8 * D_MODEL + 3 * 256 - 1) // (3 * 256) * 256
RET_CHUNK = 128
MOBA_BLOCK = 256
MOBA_TOPK = 3
MOBA_QCHUNK = 64
ROPE_BASE = 10000.0
NORM_EPS = 1e-6
GN_EPS = 1e-5

kernel_name = "hymba_retention_moba_decode_step"


def rms_norm(x, g):
    xf = x.astype(jnp.float32)
    y = xf * lax.rsqrt(jnp.mean(xf * xf, axis=-1, keepdims=True) + NORM_EPS)
    return (y * g.astype(jnp.float32)).astype(x.dtype)


def rotary(x, pos):
    half = HEAD_DIM // 2
    inv = ROPE_BASE ** (-jnp.arange(half, dtype=jnp.float32) / half)
    ang = pos.astype(jnp.float32)[:, None] * inv[None, :]
    cos = jnp.cos(ang)[None, :, None, :]
    sin = jnp.sin(ang)[None, :, None, :]
    x1, x2 = x[..., :half], x[..., half:]
    return jnp.concatenate([x1 * cos - x2 * sin, x1 * sin + x2 * cos], axis=-1)


def retention_log_decay():
    return jnp.log1p(-jnp.exp2(-5.0 - jnp.arange(H_RET, dtype=jnp.float32)))


def retention(q, k, v, s0):
    B, L, H, D = q.shape
    C = math.gcd(L, RET_CHUNK)
    nc = L // C
    lg = retention_log_decay()
    i = jnp.arange(C, dtype=jnp.float32)
    rel = i[:, None] - i[None, :]
    intra = jnp.exp(jnp.where(rel[None] >= 0, rel[None] * lg[:, None, None], -jnp.inf))
    q_dec = jnp.exp((i + 1.0)[:, None] * lg[None, :])
    k_dec = jnp.exp((C - 1.0 - i)[:, None] * lg[None, :])
    c_dec = jnp.exp(C * lg)

    def to_chunks(t):
        return t.reshape(B, nc, C, H, D).transpose(1, 0, 2, 3, 4)

    def step(s, inp):
        qc, kc, vc = inp
        sc = jnp.einsum('bihd,bjhd->bhij', qc, kc) * intra[None]
        o = jnp.einsum('bhij,bjhe->bihe', sc, vc)
        o = o + jnp.einsum('bihd,bhde->bihe', qc * q_dec[None, :, :, None], s)
        s = s * c_dec[None, :, None, None] + jnp.einsum('bjhd,bjhe->bhde', kc * k_dec[None, :, :, None], vc)
        return s, o

    s_fin, o = lax.scan(step, s0.astype(jnp.float32), (to_chunks(q), to_chunks(k), to_chunks(v)))
    return o.transpose(1, 0, 2, 3, 4).reshape(B, L, H, D), s_fin


def head_group_norm(o, g):
    mu = jnp.mean(o, axis=-1, keepdims=True)
    var = jnp.mean(jnp.square(o - mu), axis=-1, keepdims=True)
    return (o - mu) * lax.rsqrt(var + GN_EPS) * g.astype(jnp.float32).reshape(H_RET, HEAD_DIM)


def moba_attention(q, k, v, q_pos0):
    B, L, H, D = q.shape
    T = k.shape[1]
    nb = -(-T // MOBA_BLOCK)
    pad = nb * MOBA_BLOCK - T
    kb = jnp.pad(k, ((0, 0), (0, pad), (0, 0), (0, 0))).reshape(B, nb, MOBA_BLOCK, H, D)
    vb = jnp.pad(v, ((0, 0), (0, pad), (0, 0), (0, 0))).reshape(B, nb, MOBA_BLOCK, H, D)
    k_mean = jnp.mean(kb.astype(jnp.float32), axis=2)
    qpos = q_pos0 + jnp.arange(L)
    qblk = qpos // MOBA_BLOCK
    gate = jnp.einsum('blhd,bnhd->blhn', q.astype(jnp.float32), k_mean)
    past = jnp.arange(nb)[None, :] < qblk[:, None]
    gate = jnp.where(past[None, :, None, :], gate, -jnp.inf)
    n_sel = min(MOBA_TOPK, nb)
    _, top_idx = lax.top_k(gate, n_sel)
    own = jnp.broadcast_to(qblk[None, :, None, None], (B, L, H, 1)).astype(top_idx.dtype)
    sel = jnp.concatenate([top_idx, own], axis=-1)
    rank_ok = jnp.concatenate([jnp.arange(n_sel)[None, :] < qblk[:, None],
                               jnp.ones((L, 1), dtype=bool)], axis=-1)
    scale = HEAD_DIM ** -0.5
    QC = math.gcd(L, MOBA_QCHUNK)
    nq = L // QC
    b_ids = jnp.repeat(jnp.arange(B), nq)
    c_ids = jnp.tile(jnp.arange(nq), B)
    h_idx = jnp.arange(H)[None, :, None]

    def per_chunk(args):
        b, c = args
        q_c = lax.dynamic_slice_in_dim(q[b], c * QC, QC, axis=0)
        sel_c = lax.dynamic_slice_in_dim(sel[b], c * QC, QC, axis=0)
        ok_c = lax.dynamic_slice_in_dim(rank_ok, c * QC, QC, axis=0)
        pos_c = q_pos0 + c * QC + jnp.arange(QC)
        k_sel = kb[b][sel_c, :, h_idx]
        v_sel = vb[b][sel_c, :, h_idx]
        key_pos = sel_c[..., None] * MOBA_BLOCK + jnp.arange(MOBA_BLOCK)
        mask = ok_c[:, None, :, None] & (key_pos <= pos_c[:, None, None, None])
        s = jnp.einsum('qhd,qhrjd->qhrj', q_c.astype(jnp.float32), k_sel) * scale
        s = jnp.where(mask, s, -jnp.inf)
        p = jax.nn.softmax(s.reshape(QC, H, -1), axis=-1).reshape(s.shape)
        o = jnp.einsum('qhrj,qhrjd->qhd', p, v_sel.astype(jnp.float32))
        return o.astype(q.dtype)

    out = lax.map(per_chunk, (b_ids, c_ids))
    return out.reshape(B, L, H, D)


def mixer(h, pos, q_pos0, ret_s0, k_past, v_past, w_in, w_out, gn_w):
    B, L, _ = h.shape
    proj = h @ w_in
    rq, rk, rv, rg, aq, ak, av = jnp.split(
        proj, [W_RET, 2 * W_RET, 3 * W_RET, 4 * W_RET, 4 * W_RET + W_ATT, 4 * W_RET + 2 * W_ATT], axis=-1)
    rq = rotary(rq.reshape(B, L, H_RET, HEAD_DIM).astype(jnp.float32), pos)
    rk = rotary(rk.reshape(B, L, H_RET, HEAD_DIM).astype(jnp.float32), pos) * (HEAD_DIM ** -0.5)
    rv = rv.reshape(B, L, H_RET, HEAD_DIM).astype(jnp.float32)
    ret_o, ret_s = retention(rq, rk, rv, ret_s0)
    ret_o = head_group_norm(ret_o, gn_w).reshape(B, L, W_RET)
    ret_o = (jax.nn.silu(rg.astype(jnp.float32)) * ret_o).astype(h.dtype)
    aq = aq.reshape(B, L, H_ATT, HEAD_DIM)
    ak = ak.reshape(B, L, H_ATT, HEAD_DIM)
    av = av.reshape(B, L, H_ATT, HEAD_DIM)
    if k_past is None:
        k_all, v_all = ak, av
    else:
        k_all = jnp.concatenate([k_past.astype(ak.dtype), ak], axis=1)
        v_all = jnp.concatenate([v_past.astype(av.dtype), av], axis=1)
    att_o = moba_attention(aq, k_all, v_all, q_pos0).reshape(B, L, W_ATT).astype(h.dtype)
    y = jnp.concatenate([ret_o, att_o], axis=-1) @ w_out
    return y, ret_s, ak, av


def swiglu(h, w_gate, w_up, w_down):
    return (jax.nn.silu(h @ w_gate) * (h @ w_up)) @ w_down


def setup_inputs(seed: int = 0) -> dict:
    key = jax.random.key(seed)
    ks = jax.random.split(key, 16)
    n_pages = PAST_LEN // PAGE_SIZE
    n_used = DEC_BATCH * n_pages
    n_pool = n_used + max(1, n_used // 4)
    nrm = jax.random.normal
    f32 = jnp.float32
    x_prompt = nrm(ks[0], (BATCH, SEQ, D_MODEL), f32)
    x_sample = nrm(ks[1], (DEC_BATCH, DEC_SEQ, D_MODEL), f32)
    cache_k = nrm(ks[2], (DEPTH, n_pool, PAGE_SIZE, H_ATT, HEAD_DIM), f32)
    cache_v = nrm(ks[3], (DEPTH, n_pool, PAGE_SIZE, H_ATT, HEAD_DIM), f32)
    state_ret = 0.5 * nrm(ks[4], (DEPTH, DEC_BATCH, H_RET, HEAD_DIM, HEAD_DIM), f32)
    page_table = jax.random.permutation(ks[5], n_pool)[:n_used].reshape(DEC_BATCH, n_pages).astype(jnp.int32)
    w_in = nrm(ks[6], (DEPTH, D_MODEL, IN_COLS), f32) * (D_MODEL ** -0.5)
    w_out = nrm(ks[7], (DEPTH, MIX_WIDTH, D_MODEL), f32) * (MIX_WIDTH ** -0.5)
    norm_mix = 1.0 + 0.02 * nrm(ks[8], (DEPTH, D_MODEL), f32)
    ret_gn = 1.0 + 0.02 * nrm(ks[9], (DEPTH, W_RET), f32)
    norm_ffn = 1.0 + 0.02 * nrm(ks[10], (DEPTH, D_MODEL), f32)
    w_gate = nrm(ks[11], (DEPTH, D_MODEL, D_FF), f32) * (D_MODEL ** -0.5)
    w_up = nrm(ks[12], (DEPTH, D_MODEL, D_FF), f32) * (D_MODEL ** -0.5)
    w_down = nrm(ks[13], (DEPTH, D_FF, D_MODEL), f32) * (D_FF ** -0.5)
    norm_final = 1.0 + 0.02 * nrm(ks[14], (D_MODEL,), f32)
    return {"x_prompt": x_prompt, "x_sample": x_sample, "cache_k": cache_k, "cache_v": cache_v,
            "state_ret": state_ret, "page_table": page_table, "w_in": w_in, "w_out": w_out,
            "norm_mix": norm_mix, "ret_gn": ret_gn, "norm_ffn": norm_ffn, "w_gate": w_gate,
            "w_up": w_up, "w_down": w_down, "norm_final": norm_final}


def reference(x_prompt, x_sample, cache_k, cache_v, state_ret, page_table, w_in, w_out,
              norm_mix, ret_gn, norm_ffn, w_gate, w_up, w_down, norm_final):
    B, S, _ = x_prompt.shape
    DB, L, _ = x_sample.shape
    n_pages = page_table.shape[1]
    past_len = n_pages * PAGE_SIZE
    pos_p = jnp.arange(S)
    pos_s = past_len + jnp.arange(L)
    s0_p = jnp.zeros((B, H_RET, HEAD_DIM, HEAD_DIM), jnp.float32)
    yp, ys = x_prompt, x_sample
    kp_rows, vp_rows, rp_states = [], [], []
    ks_rows, vs_rows, rs_states = [], [], []
    for l in range(DEPTH):
        hp = rms_norm(yp, norm_mix[l])
        mp, sp, kp, vp = mixer(hp, pos_p, 0, s0_p, None, None, w_in[l], w_out[l], ret_gn[l])
        k_past = cache_k[l, page_table].reshape(DB, past_len, H_ATT, HEAD_DIM)
        v_past = cache_v[l, page_table].reshape(DB, past_len, H_ATT, HEAD_DIM)
        hs = rms_norm(ys, norm_mix[l])
        ms, ss, ksn, vsn = mixer(hs, pos_s, past_len, state_ret[l], k_past, v_past,
                                 w_in[l], w_out[l], ret_gn[l])
        yp = yp + mp
        ys = ys + ms
        yp = yp + swiglu(rms_norm(yp, norm_ffn[l]), w_gate[l], w_up[l], w_down[l])
        ys = ys + swiglu(rms_norm(ys, norm_ffn[l]), w_gate[l], w_up[l], w_down[l])
        kp_rows.append(kp)
        vp_rows.append(vp)
        rp_states.append(sp)
        ks_rows.append(ksn)
        vs_rows.append(vsn)
        rs_states.append(ss)
    y_prompt = rms_norm(yp, norm_final)
    y_sample = rms_norm(ys, norm_final)
    return (y_prompt, y_sample, jnp.stack(kp_rows), jnp.stack(vp_rows), jnp.stack(rp_states),
            jnp.stack(ks_rows), jnp.stack(vs_rows), jnp.stack(rs_states))
```

```python
import functools
import math

import jax
import jax.numpy as jnp
from jax import lax
from jax.experimental import pallas as pl
from jax.experimental.pallas import tpu as pltpu

HEAD_DIM = 128
RET_CHUNK = 128
MOBA_BLOCK = 256
MOBA_TOPK = 3
ROPE_BASE = 10000.0
NORM_EPS = 1e-6
GN_EPS = 1e-5

BF16 = jnp.bfloat16
F32 = jnp.float32
NEG_INF = float("-inf")
MIB = 1024 * 1024
V7X_VMEM_BUDGET = 56 * MIB
SAMPLE_ROWS = 16
KMEAN_PAGES_PER_STEP = 16


def _params(vmem_bytes, n_axes):
    return pltpu.CompilerParams(
        dimension_semantics=("arbitrary",) * n_axes,
        vmem_limit_bytes=int(min(max(vmem_bytes, 16 * MIB), V7X_VMEM_BUDGET)))


def _dot(a, b):
    return jnp.dot(a, b, preferred_element_type=F32)


def _dot_nt(a, b, precision=None):
    return lax.dot_general(a, b, (((1,), (1,)), ((), ())), precision=precision,
                           preferred_element_type=F32)


def _dot_tn(a, b):
    return lax.dot_general(a, b, (((0,), (0,)), ((), ())), preferred_element_type=F32)


def _silu(x):
    return x * (1.0 / (1.0 + jnp.exp(-x)))


def _rmsnorm_kernel(x_ref, g_ref, o_ref):
    x = x_ref[...]
    y = x * lax.rsqrt(jnp.mean(x * x, axis=-1, keepdims=True) + NORM_EPS)
    o_ref[...] = (y * g_ref[...]).astype(o_ref.dtype)


def _rmsnorm(x, g, out_dtype, tm):
    m, d = x.shape
    return pl.pallas_call(
        _rmsnorm_kernel,
        grid=(m // tm,),
        in_specs=[pl.BlockSpec((tm, d), lambda i: (i, 0)),
                  pl.BlockSpec((1, d), lambda i: (0, 0))],
        out_specs=pl.BlockSpec((tm, d), lambda i: (i, 0)),
        out_shape=jax.ShapeDtypeStruct((m, d), out_dtype),
        compiler_params=_params(6 * tm * d * 4, 1),
        name="rmsnorm",
    )(x, g.reshape(1, d))


def _proj_kernel(a_ref, w_ref, o_ref):
    o_ref[...] = _dot(a_ref[...], w_ref[...]).astype(o_ref.dtype)


def _proj(a, w, out_dtype, tm, tn):
    m, k = a.shape
    n = w.shape[1]
    osz = jnp.dtype(out_dtype).itemsize
    return pl.pallas_call(
        _proj_kernel,
        grid=(m // tm, n // tn),
        in_specs=[pl.BlockSpec((tm, k), lambda i, j: (i, 0)),
                  pl.BlockSpec((k, tn), lambda i, j: (0, j))],
        out_specs=pl.BlockSpec((tm, tn), lambda i, j: (i, j)),
        out_shape=jax.ShapeDtypeStruct((m, n), out_dtype),
        compiler_params=_params(2 * (tm * k * 2 + k * tn * 2 + tm * tn * osz) + 8 * MIB, 2),
        name="in_proj",
    )(a, w)


def _outproj_kernel(a1_ref, a2_ref, w_ref, r_ref, o_ref):
    k1 = a1_ref.shape[1]
    acc = _dot(a1_ref[...], w_ref[:k1, :]) + _dot(a2_ref[...], w_ref[k1:, :])
    o_ref[...] = r_ref[...] + acc


def _outproj(a1, a2, w, res, tm, tn):
    m, k1 = a1.shape
    k2 = a2.shape[1]
    n = w.shape[1]
    return pl.pallas_call(
        _outproj_kernel,
        grid=(m // tm, n // tn),
        in_specs=[pl.BlockSpec((tm, k1), lambda i, j: (i, 0)),
                  pl.BlockSpec((tm, k2), lambda i, j: (i, 0)),
                  pl.BlockSpec((k1 + k2, tn), lambda i, j: (0, j)),
                  pl.BlockSpec((tm, tn), lambda i, j: (i, j))],
        out_specs=pl.BlockSpec((tm, tn), lambda i, j: (i, j)),
        out_shape=jax.ShapeDtypeStruct((m, n), F32),
        compiler_params=_params(
            2 * (tm * (k1 + k2) * 2 + (k1 + k2) * tn * 2 + 2 * tm * tn * 4) + 8 * MIB, 2),
        name="out_proj",
    )(a1, a2, w, res)


def _gateup_kernel(h_ref, wg_ref, wu_ref, o_ref):
    h = h_ref[...]
    g = _dot(h, wg_ref[...])
    u = _dot(h, wu_ref[...])
    o_ref[...] = (_silu(g) * u).astype(o_ref.dtype)


def _gateup(h, wg, wu, tm, tn):
    m, k = h.shape
    n = wg.shape[1]
    return pl.pallas_call(
        _gateup_kernel,
        grid=(m // tm, n // tn),
        in_specs=[pl.BlockSpec((tm, k), lambda i, j: (i, 0)),
                  pl.BlockSpec((k, tn), lambda i, j: (0, j)),
                  pl.BlockSpec((k, tn), lambda i, j: (0, j))],
        out_specs=pl.BlockSpec((tm, tn), lambda i, j: (i, j)),
        out_shape=jax.ShapeDtypeStruct((m, n), BF16),
        compiler_params=_params(2 * (tm * k * 2 + 2 * k * tn * 2 + tm * tn * 2) + 8 * MIB, 2),
        name="ffn_gate_up",
    )(h, wg, wu)


def _down_kernel(a_ref, w_ref, r_ref, o_ref):
    o_ref[...] = r_ref[...] + _dot(a_ref[...], w_ref[...])


def _down(a, w, res, tm, tn):
    m, k = a.shape
    n = w.shape[1]
    return pl.pallas_call(
        _down_kernel,
        grid=(m // tm, n // tn),
        in_specs=[pl.BlockSpec((tm, k), lambda i, j: (i, 0)),
                  pl.BlockSpec((k, tn), lambda i, j: (0, j)),
                  pl.BlockSpec((tm, tn), lambda i, j: (i, j))],
        out_specs=pl.BlockSpec((tm, tn), lambda i, j: (i, j)),
        out_shape=jax.ShapeDtypeStruct((m, n), F32),
        compiler_params=_params(2 * (tm * k * 2 + k * tn * 2 + 2 * tm * tn * 4) + 8 * MIB, 2),
        name="ffn_down",
    )(a, w, res)


def _rotary_tables(pos):
    half = HEAD_DIM // 2
    inv = ROPE_BASE ** (-jnp.arange(half, dtype=F32) / half)
    ang = pos.astype(F32)[:, None] * inv[None, :]
    cos, sin = jnp.cos(ang), jnp.sin(ang)
    return (jnp.concatenate([cos, cos], axis=-1), jnp.concatenate([-sin, sin], axis=-1))


def _log_decay(h_ret):
    return jnp.log1p(-jnp.exp2(-5.0 - jnp.arange(h_ret, dtype=F32)))


def _group_norm_gate(o, gn, g):
    mu = jnp.mean(o, axis=-1, keepdims=True)
    d = o - mu
    var = jnp.mean(d * d, axis=-1, keepdims=True)
    return _silu(g) * (d * lax.rsqrt(var + GN_EPS) * gn)


def _ret_prompt_kernel(q_ref, k_ref, v_ref, g_ref, cos_ref, sin_ref, intra_ref, qdec_ref,
                       kdec_ref, cdec_ref, gn_ref, o_ref, s_ref, *, chunk, n_chunks):
    half = HEAD_DIM // 2
    scale = HEAD_DIM ** -0.5
    intra = intra_ref[0]
    qdec = qdec_ref[0]
    kdec = kdec_ref[0]
    cdec = cdec_ref[0, 0:1, :]
    gn = gn_ref[0]

    def body(c, s):
        r = pl.ds(pl.multiple_of(c * chunk, chunk), chunk)
        cos = cos_ref[r, :]
        sin = sin_ref[r, :]
        q = q_ref[r, :]
        q = q * cos + pltpu.roll(q, half, 1) * sin
        k = k_ref[r, :]
        k = (k * cos + pltpu.roll(k, half, 1) * sin) * scale
        v = v_ref[r, :].astype(BF16)
        sc = _dot_nt(q.astype(BF16), k.astype(BF16)) * intra
        o = _dot(sc.astype(BF16), v) + _dot((q * qdec).astype(BF16), s.astype(BF16))
        s_new = s * cdec + _dot_tn((k * kdec).astype(BF16), v)
        o_ref[r, :] = _group_norm_gate(o, gn, g_ref[r, :]).astype(o_ref.dtype)
        return s_new

    s_fin = lax.fori_loop(0, n_chunks, body, jnp.zeros((HEAD_DIM, HEAD_DIM), F32))
    s_ref[0, 0] = s_fin


def _ret_prompt(proj, gn_w, batch, seq, h_ret):
    chunk = math.gcd(seq, RET_CHUNK)
    n_chunks = seq // chunk
    cos2, sin2 = _rotary_tables(jnp.arange(seq))
    lg = _log_decay(h_ret)
    i = jnp.arange(chunk, dtype=F32)
    rel = i[:, None] - i[None, :]
    intra = jnp.exp(jnp.where(rel[None] >= 0, rel[None] * lg[:, None, None], -jnp.inf))
    ones = jnp.ones((1, 1, HEAD_DIM), F32)
    qdec = jnp.exp((i + 1.0)[None, :, None] * lg[:, None, None]) * ones
    kdec = jnp.exp((chunk - 1.0 - i)[None, :, None] * lg[:, None, None]) * ones
    cdec = jnp.exp(chunk * lg)[:, None, None] * jnp.ones((1, 8, HEAD_DIM), F32)
    gn = gn_w.astype(F32).reshape(h_ret, 1, HEAD_DIM)
    hd = HEAD_DIM

    def col(group):
        return pl.BlockSpec((seq, hd), lambda b, h: (b, group * h_ret + h))

    const = lambda shape: pl.BlockSpec(shape, lambda b, h: (h, 0, 0))
    return pl.pallas_call(
        functools.partial(_ret_prompt_kernel, chunk=chunk, n_chunks=n_chunks),
        grid=(batch, h_ret),
        in_specs=[col(0), col(1), col(2), col(3),
                  pl.BlockSpec((seq, hd), lambda b, h: (0, 0)),
                  pl.BlockSpec((seq, hd), lambda b, h: (0, 0)),
                  const((1, chunk, chunk)), const((1, chunk, hd)), const((1, chunk, hd)),
                  const((1, 8, hd)), const((1, 1, hd))],
        out_specs=[pl.BlockSpec((seq, hd), lambda b, h: (b, h)),
                   pl.BlockSpec((1, 1, hd, hd), lambda b, h: (b, h, 0, 0))],
        out_shape=[jax.ShapeDtypeStruct((batch * seq, h_ret * hd), BF16),
                   jax.ShapeDtypeStruct((batch, h_ret, hd, hd), F32)],
        compiler_params=_params(2 * 7 * seq * hd * 4 + 8 * MIB, 2),
        name="retention_prompt",
    )(proj, proj, proj, proj, cos2, sin2, intra, qdec, kdec, cdec, gn)


def _block_rank(gate, n_past):
    lane = lax.broadcasted_iota(jnp.int32, gate.shape, 1)
    g = jnp.where(lane < n_past, gate, NEG_INF)
    rank = jnp.zeros(gate.shape, F32)
    for s in range(1, n_past):
        for shift in (s, HEAD_DIM - s):
            g_src = pltpu.roll(g, shift, 1)
            i_src = pltpu.roll(lane, shift, 1)
            ahead = jnp.where(g_src > g, 1.0,
                              jnp.where(g_src == g, jnp.where(i_src < lane, 1.0, 0.0), 0.0))
            rank = rank + ahead
    return rank


def _moba_prompt_kernel(q_ref, k_ref, v_ref, o_ref, *, n_blocks):
    blk = MOBA_BLOCK
    scale = HEAD_DIM ** -0.5
    kf = k_ref[...]
    kb = kf.astype(BF16)
    vb = v_ref[...].astype(BF16)
    kmean = jnp.mean(kf.reshape(n_blocks, blk, HEAD_DIM), axis=1)
    kmean_pad = jnp.concatenate(
        [kmean, jnp.zeros((HEAD_DIM - n_blocks, HEAD_DIM), F32)], axis=0)
    row = lax.broadcasted_iota(jnp.int32, (blk, blk), 0)
    colv = lax.broadcasted_iota(jnp.int32, (blk, blk), 1)
    causal = colv <= row

    for t in range(n_blocks):
        rows = slice(t * blk, (t + 1) * blk)
        qf = q_ref[rows, :]
        s = _dot_nt((qf * scale).astype(BF16), kb[:(t + 1) * blk])
        pieces = []
        if t > MOBA_TOPK:
            gate = _dot_nt(qf, kmean_pad, precision=lax.Precision.HIGHEST)
            rank = _block_rank(gate, t)
        for j in range(t):
            sj = s[:, j * blk:(j + 1) * blk]
            if t > MOBA_TOPK:
                sj = jnp.where(rank[:, j:j + 1] < float(MOBA_TOPK), sj, NEG_INF)
            pieces.append(sj)
        pieces.append(jnp.where(causal, s[:, t * blk:], NEG_INF))
        s = jnp.concatenate(pieces, axis=1) if len(pieces) > 1 else pieces[0]
        m = jnp.max(s, axis=1, keepdims=True)
        p = jnp.exp(s - m)
        l = jnp.sum(p, axis=1, keepdims=True)
        o = _dot(p.astype(BF16), vb[:(t + 1) * blk]) / l
        o_ref[rows, :] = o.astype(o_ref.dtype)


def _moba_prompt(proj, batch, seq, h_att, col0):
    assert seq % MOBA_BLOCK == 0
    n_blocks = seq // MOBA_BLOCK
    assert n_blocks <= 64
    hd = HEAD_DIM

    def col(group):
        return pl.BlockSpec((seq, hd), lambda b, h: (b, col0 + group * h_att + h))

    return pl.pallas_call(
        functools.partial(_moba_prompt_kernel, n_blocks=n_blocks),
        grid=(batch, h_att),
        in_specs=[col(0), col(1), col(2)],
        out_specs=pl.BlockSpec((seq, hd), lambda b, h: (b, h)),
        out_shape=jax.ShapeDtypeStruct((batch * seq, h_att * hd), BF16),
        compiler_params=_params(32 * MIB, 2),
        name="moba_prompt",
    )(proj, proj, proj)


def _row_to_col(x_row, eye):
    n = x_row.shape[1]
    return jnp.sum(jnp.where(eye, jnp.broadcast_to(x_row, (n, n)), 0.0), axis=1, keepdims=True)


def _ret_sample_kernel(q_ref, k_ref, v_ref, g_ref, s_ref, cos_ref, sin_ref, dec_ref, gn_ref,
                       o_ref, so_ref, *, h_ret):
    half = HEAD_DIM // 2
    scale = HEAD_DIM ** -0.5
    cos = cos_ref[...]
    sin = sin_ref[...]
    q = q_ref[0]
    q = q * cos + pltpu.roll(q, half, 1) * sin
    k = k_ref[0]
    k = (k * cos + pltpu.roll(k, half, 1) * sin) * scale
    v = v_ref[0]
    r0 = lax.broadcasted_iota(jnp.int32, (HEAD_DIM, HEAD_DIM), 0)
    r1 = lax.broadcasted_iota(jnp.int32, (HEAD_DIM, HEAD_DIM), 1)
    eye = r0 == r1
    for h in range(h_ret):
        hh = slice(h, h + 1)
        kcol = _row_to_col(k[hh, :], eye)
        qcol = _row_to_col(q[hh, :], eye)
        s_new = s_ref[0, 0, h] * dec_ref[hh, :] + kcol * v[hh, :]
        so_ref[0, 0, h] = s_new
        o = jnp.sum(qcol * s_new, axis=0, keepdims=True)
        o_ref[0, hh, :] = _group_norm_gate(o, gn_ref[hh, :], g_ref[0, hh, :]).astype(o_ref.dtype)


def _ret_sample(proj3, state, layer, gn_w, pos, dec_batch, h_ret):
    hd = HEAD_DIM
    cos2, sin2 = _rotary_tables(pos)
    dec = jnp.exp(1.0 * _log_decay(h_ret))[:, None] * jnp.ones((1, hd), F32)
    gn = gn_w.astype(F32).reshape(h_ret, hd)
    grp = lambda g: pl.BlockSpec((1, h_ret, hd), lambda b: (b, g, 0))
    full = lambda shape: pl.BlockSpec(shape, lambda b: (0,) * len(shape))
    return pl.pallas_call(
        functools.partial(_ret_sample_kernel, h_ret=h_ret),
        grid=(dec_batch,),
        in_specs=[grp(0), grp(1), grp(2), grp(3),
                  pl.BlockSpec((1, 1, h_ret, hd, hd), lambda b: (layer, b, 0, 0, 0)),
                  full((1, hd)), full((1, hd)), full((h_ret, hd)), full((h_ret, hd))],
        out_specs=[pl.BlockSpec((1, h_ret, hd), lambda b: (b, 0, 0)),
                   pl.BlockSpec((1, 1, h_ret, hd, hd), lambda b: (0, b, 0, 0, 0))],
        out_shape=[jax.ShapeDtypeStruct((dec_batch, h_ret, hd), F32),
                   jax.ShapeDtypeStruct((1, dec_batch, h_ret, hd, hd), F32)],
        compiler_params=_params(16 * MIB, 1),
        name="retention_sample",
    )(proj3, proj3, proj3, proj3, state, cos2, sin2, dec, gn)


def _kmean_kernel(pt_ref, *refs, pages_per_block):
    del pt_ref
    o_ref = refs[-1]
    pages = refs[:-1]
    n_out = len(pages) // pages_per_block
    n_heads = pages[0].shape[3]
    inv_n = 1.0 / (pages[0].shape[2] * pages_per_block)
    for i in range(n_out):
        acc = jnp.sum(pages[i * pages_per_block][0, 0], axis=0)
        for p in range(1, pages_per_block):
            acc = acc + jnp.sum(pages[i * pages_per_block + p][0, 0], axis=0)
        acc = acc * inv_n
        for h in range(n_heads):
            o_ref[0, h, i:i + 1, :] = acc[h:h + 1, :]


def _kmean_sample(cache, layer, page_table, pages_per_block):
    db, n_pages = page_table.shape
    page, n_heads, hd = cache.shape[2:]
    pps = KMEAN_PAGES_PER_STEP
    assert n_pages % pps == 0 and pps % pages_per_block == 0
    blocks_per_step = pps // pages_per_block
    assert blocks_per_step % 8 == 0

    def page_spec(i):
        return pl.BlockSpec((1, 1, page, n_heads, hd),
                            lambda b, c, pt: (layer, pt[b, c * pps + i], 0, 0, 0))

    grid_spec = pltpu.PrefetchScalarGridSpec(
        num_scalar_prefetch=1,
        grid=(db, n_pages // pps),
        in_specs=[page_spec(i) for i in range(pps)],
        out_specs=pl.BlockSpec((1, n_heads, blocks_per_step, hd), lambda b, c, pt: (b, 0, c, 0)),
    )
    return pl.pallas_call(
        functools.partial(_kmean_kernel, pages_per_block=pages_per_block),
        grid_spec=grid_spec,
        out_shape=jax.ShapeDtypeStruct((db, n_heads, n_pages // pages_per_block, hd), F32),
        compiler_params=_params(2 * pps * page * n_heads * hd * 4 + 8 * MIB, 2),
        name="moba_sample_kmean",
    )(page_table, *([cache] * pps))


def _select_kernel(km_ref, q_ref, o_ref, *, n_sel):
    n_heads, nb = km_ref.shape[1], km_ref.shape[2]
    blk_id = lax.broadcasted_iota(jnp.int32, (nb, 1), 0).astype(F32)
    row = lax.broadcasted_iota(jnp.int32, o_ref.shape[1:], 0)
    lane = lax.broadcasted_iota(jnp.int32, o_ref.shape[1:], 1)
    out = jnp.zeros(o_ref.shape[1:], jnp.int32)
    for h in range(n_heads):
        gate = jnp.sum(km_ref[0, h] * q_ref[0, h:h + 1, :], axis=1, keepdims=True)
        for r in range(n_sel):
            best = jnp.max(gate, axis=0, keepdims=True)
            idx = jnp.min(jnp.where(gate == best, blk_id, float(nb)), axis=0, keepdims=True)
            out = jnp.where((row == r) & (lane == h), idx.astype(jnp.int32), out)
            gate = jnp.where(blk_id == idx, NEG_INF, gate)
    o_ref[0] = out


def _select_sample(kmean, proj3, q_group, n_sel):
    db, n_heads, nb, hd = kmean.shape
    assert n_sel <= 8 and n_heads <= hd
    return pl.pallas_call(
        functools.partial(_select_kernel, n_sel=n_sel),
        grid=(db,),
        in_specs=[pl.BlockSpec((1, n_heads, nb, hd), lambda b: (b, 0, 0, 0)),
                  pl.BlockSpec((1, n_heads, hd), lambda b: (b, q_group, 0))],
        out_specs=pl.BlockSpec((1, 8, hd), lambda b: (b, 0, 0)),
        out_shape=jax.ShapeDtypeStruct((db, 8, hd), jnp.int32),
        compiler_params=_params(16 * MIB, 1),
        name="moba_sample_select",
    )(kmean, proj3)


def _attn_sample_kernel(pt_ref, sel_ref, q_ref, kn_ref, vn_ref, *refs, n_pages_sel, head_tile):
    del pt_ref, sel_ref
    o_ref = refs[-1]
    k_pages = refs[:n_pages_sel]
    v_pages = refs[n_pages_sel:2 * n_pages_sel]
    hd = HEAD_DIM
    scale = hd ** -0.5
    page = k_pages[0].shape[2]
    rows = page * head_tile
    k_all = jnp.concatenate([kp[0, 0].reshape(rows, hd) for kp in k_pages], axis=0)
    v_all = jnp.concatenate([vp[0, 0].reshape(rows, hd) for vp in v_pages], axis=0)
    q = q_ref[0]
    q8 = jnp.broadcast_to(q, (8, hd))
    s = _dot_nt(q8, k_all, precision=lax.Precision.HIGHEST) * scale
    head_in_tile = pl.program_id(1) % head_tile
    col = lax.broadcasted_iota(jnp.int32, s.shape, 1)
    s = jnp.where(col % head_tile == head_in_tile, s, NEG_INF)
    s_own = jnp.sum(kn_ref[0] * q, axis=1, keepdims=True) * scale
    m = jnp.maximum(jnp.max(s, axis=1, keepdims=True), s_own)
    e = jnp.exp(s - m)
    e_own = jnp.exp(s_own - m)
    l = jnp.sum(e, axis=1, keepdims=True) + e_own
    o = jnp.dot(e, v_all, precision=lax.Precision.HIGHEST,
                preferred_element_type=F32) + e_own * vn_ref[0]
    o_ref[0] = (o / l)[0:1, :]


def _attn_sample(cache_k, cache_v, layer, page_table, sel2, proj4, groups, dec_batch,
                 n_sel, pages_per_block):
    page, h_att, hd = cache_k.shape[2:]
    head_tile = 8
    assert h_att % head_tile == 0
    n_groups, q_group = groups
    n_pages_sel = n_sel * pages_per_block

    def vec_spec(group):
        return pl.BlockSpec(
            (1, 1, hd), lambda b, h, pt, sel: ((b * n_groups + group) * h_att + h, 0, 0))

    def page_spec(r, i):
        return pl.BlockSpec(
            (1, 1, page, head_tile, hd),
            lambda b, h, pt, sel: (layer, pt[b, sel[b * 8 + r, h] * pages_per_block + i], 0,
                                   h // head_tile, 0))

    page_specs = [page_spec(r, i) for r in range(n_sel) for i in range(pages_per_block)]
    grid_spec = pltpu.PrefetchScalarGridSpec(
        num_scalar_prefetch=2,
        grid=(dec_batch, h_att),
        in_specs=[vec_spec(q_group), vec_spec(q_group + 1), vec_spec(q_group + 2)]
        + page_specs + page_specs,
        out_specs=pl.BlockSpec((1, 1, hd), lambda b, h, pt, sel: (b * h_att + h, 0, 0)),
    )
    return pl.pallas_call(
        functools.partial(_attn_sample_kernel, n_pages_sel=n_pages_sel, head_tile=head_tile),
        grid_spec=grid_spec,
        out_shape=jax.ShapeDtypeStruct((dec_batch * h_att, 1, hd), F32),
        compiler_params=_params(
            4 * n_pages_sel * page * head_tile * hd * 4 + 16 * MIB, 2),
        name="moba_sample_attn",
    )(page_table, sel2, proj4, proj4, proj4,
      *([cache_k] * n_pages_sel), *([cache_v] * n_pages_sel))


def _pad_rows(x, rows):
    return jnp.pad(x, ((0, rows - x.shape[0]), (0, 0)))


def kernel(x_prompt, x_sample, cache_k, cache_v, state_ret, page_table, w_in, w_out, norm_mix,
           ret_gn, norm_ffn, w_gate, w_up, w_down, norm_final):
    batch, seq, d_model = x_prompt.shape
    dec_batch, dec_seq, _ = x_sample.shape
    depth, n_pool, page_size, h_att, hd = cache_k.shape
    h_ret = state_ret.shape[2]
    w_ret, w_att = h_ret * hd, h_att * hd
    n_pages = page_table.shape[1]
    past_len = n_pages * page_size
    assert hd == HEAD_DIM and h_ret == h_att and dec_seq == 1
    assert MOBA_BLOCK % page_size == 0 and past_len % MOBA_BLOCK == 0
    pages_per_block = MOBA_BLOCK // page_size
    n_past_blocks = past_len // MOBA_BLOCK
    assert n_past_blocks >= MOBA_TOPK and dec_batch <= SAMPLE_ROWS
    n_groups = 7
    att_group = 4
    in_cols = w_in.shape[2]
    assert in_cols == n_groups * w_ret

    m_p = batch * seq
    yp = x_prompt.reshape(m_p, d_model)
    ys = _pad_rows(x_sample.reshape(dec_batch, d_model), SAMPLE_ROWS)
    pos_s = past_len + jnp.arange(dec_seq)

    w_in_b, w_out_b = w_in.astype(BF16), w_out.astype(BF16)
    w_gate_b, w_up_b, w_down_b = w_gate.astype(BF16), w_up.astype(BF16), w_down.astype(BF16)
    d_ff = w_gate.shape[2]
    tn_ff = 256
    assert d_ff % tn_ff == 0

    kp_rows, vp_rows, rp_states, ks_rows, vs_rows, rs_states = [], [], [], [], [], []
    for l in range(depth):
        hp = _rmsnorm(yp, norm_mix[l], BF16, 256)
        proj_p = _proj(hp, w_in_b[l], F32, 1024, 1024)
        ret_o, ret_s = _ret_prompt(proj_p, ret_gn[l], batch, seq, h_ret)
        att_o = _moba_prompt(proj_p, batch, seq, h_att, att_group * h_ret)
        kp_rows.append(proj_p[:, 5 * w_ret:6 * w_ret].reshape(batch, seq, h_att, hd))
        vp_rows.append(proj_p[:, 6 * w_ret:7 * w_ret].reshape(batch, seq, h_att, hd))
        rp_states.append(ret_s)
        yp = _outproj(ret_o, att_o, w_out_b[l], yp, 1024, 1024)
        hf = _rmsnorm(yp, norm_ffn[l], BF16, 256)
        act = _gateup(hf, w_gate_b[l], w_up_b[l], 1024, tn_ff)
        yp = _down(act, w_down_b[l], yp, 512, 512)

        hs = _rmsnorm(ys, norm_mix[l], BF16, SAMPLE_ROWS)
        proj_s = _proj(hs, w_in_b[l], F32, SAMPLE_ROWS, 1024)
        proj3 = proj_s.reshape(SAMPLE_ROWS, n_groups * h_ret, hd)
        proj4 = proj_s.reshape(SAMPLE_ROWS * n_groups * h_ret, 1, hd)
        ret_os, ret_ss = _ret_sample(proj3, state_ret, l, ret_gn[l], pos_s, dec_batch, h_ret)
        kmean = _kmean_sample(cache_k, l, page_table, pages_per_block)
        sel = _select_sample(kmean, proj3, att_group, MOBA_TOPK)
        att_os = _attn_sample(cache_k, cache_v, l, page_table, sel.reshape(dec_batch * 8, hd),
                              proj4, (n_groups, att_group), dec_batch, MOBA_TOPK,
                              pages_per_block)
        ks_rows.append(proj_s[:dec_batch, 5 * w_ret:6 * w_ret].reshape(dec_batch, 1, h_att, hd))
        vs_rows.append(proj_s[:dec_batch, 6 * w_ret:7 * w_ret].reshape(dec_batch, 1, h_att, hd))
        rs_states.append(ret_ss[0])
        mix_r = _pad_rows(ret_os.reshape(dec_batch, w_ret), SAMPLE_ROWS).astype(BF16)
        mix_a = _pad_rows(att_os.reshape(dec_batch, w_att), SAMPLE_ROWS).astype(BF16)
        ys = _outproj(mix_r, mix_a, w_out_b[l], ys, SAMPLE_ROWS, 1024)
        hfs = _rmsnorm(ys, norm_ffn[l], BF16, SAMPLE_ROWS)
        acts = _gateup(hfs, w_gate_b[l], w_up_b[l], SAMPLE_ROWS, tn_ff)
        ys = _down(acts, w_down_b[l], ys, SAMPLE_ROWS, 512)

    y_prompt = _rmsnorm(yp, norm_final, F32, 256).reshape(batch, seq, d_model)
    y_sample = _rmsnorm(ys, norm_final, F32, SAMPLE_ROWS)[:dec_batch].reshape(dec_batch, 1, d_model)
    return (y_prompt, y_sample, jnp.stack(kp_rows), jnp.stack(vp_rows), jnp.stack(rp_states),
            jnp.stack(ks_rows), jnp.stack(vs_rows), jnp.stack(rs_states))
```

```python
import functools
import math

import jax
import jax.numpy as jnp
from jax import lax
from jax.experimental import pallas as pl
from jax.experimental.pallas import tpu as pltpu

HEAD_DIM = 128
RET_CHUNK = 128
MOBA_BLOCK = 256
MOBA_TOPK = 3
ROPE_BASE = 10000.0
NORM_EPS = 1e-6
GN_EPS = 1e-5

BF16 = jnp.bfloat16
F32 = jnp.float32
NEG_INF = float("-inf")
MIB = 1024 * 1024
V7X_VMEM_BUDGET = 56 * MIB
SAMPLE_ROWS = 16
KMEAN_PAGES_PER_STEP = 16


def _params(vmem_bytes, n_axes):
    return pltpu.CompilerParams(
        dimension_semantics=("arbitrary",) * n_axes,
        vmem_limit_bytes=int(min(max(vmem_bytes, 16 * MIB), V7X_VMEM_BUDGET)))


def _dot(a, b):
    return jnp.dot(a, b, preferred_element_type=F32)


def _dot_nt(a, b, precision=None):
    return lax.dot_general(a, b, (((1,), (1,)), ((), ())), precision=precision,
                           preferred_element_type=F32)


def _dot_tn(a, b):
    return lax.dot_general(a, b, (((0,), (0,)), ((), ())), preferred_element_type=F32)


def _silu(x):
    return x * (1.0 / (1.0 + jnp.exp(-x)))


def _rmsnorm_kernel(x_ref, g_ref, o_ref):
    x = x_ref[...]
    y = x * lax.rsqrt(jnp.mean(x * x, axis=-1, keepdims=True) + NORM_EPS)
    o_ref[...] = (y * g_ref[...]).astype(o_ref.dtype)


def _rmsnorm(x, g, out_dtype, tm):
    m, d = x.shape
    return pl.pallas_call(
        _rmsnorm_kernel,
        grid=(m // tm,),
        in_specs=[pl.BlockSpec((tm, d), lambda i: (i, 0)),
                  pl.BlockSpec((1, d), lambda i: (0, 0))],
        out_specs=pl.BlockSpec((tm, d), lambda i: (i, 0)),
        out_shape=jax.ShapeDtypeStruct((m, d), out_dtype),
        compiler_params=_params(6 * tm * d * 4, 1),
        name="rmsnorm",
    )(x, g.reshape(1, d))


def _proj_kernel(a_ref, w_ref, o_ref):
    o_ref[...] = _dot(a_ref[...], w_ref[...]).astype(o_ref.dtype)


def _proj(a, w, out_dtype, tm, tn, n_cols=None):
    m, k = a.shape
    n = w.shape[1] if n_cols is None else n_cols
    assert m % tm == 0 and n % tn == 0
    osz = jnp.dtype(out_dtype).itemsize
    return pl.pallas_call(
        _proj_kernel,
        grid=(m // tm, n // tn),
        in_specs=[pl.BlockSpec((tm, k), lambda i, j: (i, 0)),
                  pl.BlockSpec((k, tn), lambda i, j: (0, j))],
        out_specs=pl.BlockSpec((tm, tn), lambda i, j: (i, j)),
        out_shape=jax.ShapeDtypeStruct((m, n), out_dtype),
        compiler_params=_params(2 * (tm * k * 2 + k * tn * 2 + tm * tn * osz) + 8 * MIB, 2),
        name="in_proj",
    )(a, w)


def _kv_proj_kernel(*refs):
    a_ref, w_ref = refs[0], refs[1]
    o5_ref, o2_ref = refs[-2], refs[-1]
    acc = _dot(a_ref[...], w_ref[...])
    o2_ref[...] = acc.astype(o2_ref.dtype)
    for h in range(o5_ref.shape[3]):
        o5_ref[0, 0, :, h, :] = acc[:, h * HEAD_DIM:(h + 1) * HEAD_DIM]


def _kv_proj(a, w, col0, layer, rows5, prev, tm):
    m, k = a.shape
    depth, batch, seq, n_heads, hd = rows5
    heads_per_step = 8
    tn = heads_per_step * hd
    assert seq % tm == 0 and n_heads % heads_per_step == 0 and col0 % tn == 0
    tiles_per_seq = seq // tm
    in_specs = [pl.BlockSpec((tm, k), lambda i, j: (i, 0)),
                pl.BlockSpec((k, tn), lambda i, j: (0, col0 // tn + j))]
    args = [a, w]
    aliases = {}
    if prev is not None:
        in_specs.append(pl.BlockSpec(memory_space=pl.ANY))
        args.append(prev)
        aliases = {2: 0}
    return pl.pallas_call(
        _kv_proj_kernel,
        grid=(m // tm, n_heads // heads_per_step),
        in_specs=in_specs,
        out_specs=[pl.BlockSpec((1, 1, tm, heads_per_step, hd),
                                lambda i, j: (layer, i // tiles_per_seq, i % tiles_per_seq, j, 0)),
                   pl.BlockSpec((tm, tn), lambda i, j: (i, j))],
        out_shape=[jax.ShapeDtypeStruct(rows5, F32),
                   jax.ShapeDtypeStruct((m, n_heads * hd), BF16)],
        input_output_aliases=aliases,
        compiler_params=_params(2 * (tm * k * 2 + k * tn * 2 + tm * tn * 6) + 8 * MIB, 2),
        name="kv_proj",
    )(*args)


def _outproj_kernel(a1_ref, a2_ref, w_ref, r_ref, o_ref):
    k1 = a1_ref.shape[1]
    acc = _dot(a1_ref[...], w_ref[:k1, :]) + _dot(a2_ref[...], w_ref[k1:, :])
    o_ref[...] = r_ref[...] + acc


def _outproj(a1, a2, w, res, tm, tn):
    m, k1 = a1.shape
    k2 = a2.shape[1]
    n = w.shape[1]
    return pl.pallas_call(
        _outproj_kernel,
        grid=(m // tm, n // tn),
        in_specs=[pl.BlockSpec((tm, k1), lambda i, j: (i, 0)),
                  pl.BlockSpec((tm, k2), lambda i, j: (i, 0)),
                  pl.BlockSpec((k1 + k2, tn), lambda i, j: (0, j)),
                  pl.BlockSpec((tm, tn), lambda i, j: (i, j))],
        out_specs=pl.BlockSpec((tm, tn), lambda i, j: (i, j)),
        out_shape=jax.ShapeDtypeStruct((m, n), F32),
        compiler_params=_params(
            2 * (tm * (k1 + k2) * 2 + (k1 + k2) * tn * 2 + 2 * tm * tn * 4) + 8 * MIB, 2),
        name="out_proj",
    )(a1, a2, w, res)


def _gateup_kernel(h_ref, wg_ref, wu_ref, o_ref):
    h = h_ref[...]
    g = _dot(h, wg_ref[...])
    u = _dot(h, wu_ref[...])
    o_ref[...] = (_silu(g) * u).astype(o_ref.dtype)


def _gateup(h, wg, wu, tm, tn):
    m, k = h.shape
    n = wg.shape[1]
    return pl.pallas_call(
        _gateup_kernel,
        grid=(m // tm, n // tn),
        in_specs=[pl.BlockSpec((tm, k), lambda i, j: (i, 0)),
                  pl.BlockSpec((k, tn), lambda i, j: (0, j)),
                  pl.BlockSpec((k, tn), lambda i, j: (0, j))],
        out_specs=pl.BlockSpec((tm, tn), lambda i, j: (i, j)),
        out_shape=jax.ShapeDtypeStruct((m, n), BF16),
        compiler_params=_params(2 * (tm * k * 2 + 2 * k * tn * 2 + tm * tn * 2) + 8 * MIB, 2),
        name="ffn_gate_up",
    )(h, wg, wu)


def _down_kernel(a_ref, w_ref, r_ref, o_ref):
    o_ref[...] = r_ref[...] + _dot(a_ref[...], w_ref[...])


def _down(a, w, res, tm, tn):
    m, k = a.shape
    n = w.shape[1]
    return pl.pallas_call(
        _down_kernel,
        grid=(m // tm, n // tn),
        in_specs=[pl.BlockSpec((tm, k), lambda i, j: (i, 0)),
                  pl.BlockSpec((k, tn), lambda i, j: (0, j)),
                  pl.BlockSpec((tm, tn), lambda i, j: (i, j))],
        out_specs=pl.BlockSpec((tm, tn), lambda i, j: (i, j)),
        out_shape=jax.ShapeDtypeStruct((m, n), F32),
        compiler_params=_params(2 * (tm * k * 2 + k * tn * 2 + 2 * tm * tn * 4) + 8 * MIB, 2),
        name="ffn_down",
    )(a, w, res)


def _rotary_tables(pos):
    half = HEAD_DIM // 2
    inv = ROPE_BASE ** (-jnp.arange(half, dtype=F32) / half)
    ang = pos.astype(F32)[:, None] * inv[None, :]
    cos, sin = jnp.cos(ang), jnp.sin(ang)
    return (jnp.concatenate([cos, cos], axis=-1), jnp.concatenate([-sin, sin], axis=-1))


def _log_decay(h_ret):
    return jnp.log1p(-jnp.exp2(-5.0 - jnp.arange(h_ret, dtype=F32)))


def _group_norm_gate(o, gn, g):
    mu = jnp.mean(o, axis=-1, keepdims=True)
    d = o - mu
    var = jnp.mean(d * d, axis=-1, keepdims=True)
    return _silu(g) * (d * lax.rsqrt(var + GN_EPS) * gn)


def _ret_prompt_kernel(q_ref, k_ref, v_ref, g_ref, cos_ref, sin_ref, intra_ref, qdec_ref,
                       kdec_ref, cdec_ref, gn_ref, o_ref, s_ref,
                       qb_ref, kb_ref, qd_ref, kd_ref, vb_ref, of_ref, *, chunk, n_chunks, unroll):
    half = HEAD_DIM // 2
    scale = HEAD_DIM ** -0.5
    seq = q_ref.shape[0]
    intra = intra_ref[0]
    cdec = cdec_ref[0, 0:1, :]

    cos = cos_ref[...]
    sin = sin_ref[...]
    q = q_ref[...]
    q = q * cos + pltpu.roll(q, half, 1) * sin
    k = k_ref[...]
    k = (k * cos + pltpu.roll(k, half, 1) * sin) * scale
    qb_ref[...] = q.astype(BF16)
    kb_ref[...] = k.astype(BF16)
    qd = q.reshape(n_chunks, chunk, HEAD_DIM) * qdec_ref[...]
    kd = k.reshape(n_chunks, chunk, HEAD_DIM) * kdec_ref[...]
    qd_ref[...] = qd.reshape(seq, HEAD_DIM).astype(BF16)
    kd_ref[...] = kd.reshape(seq, HEAD_DIM).astype(BF16)
    vb_ref[...] = v_ref[...].astype(BF16)

    def body(c, s):
        r = pl.ds(pl.multiple_of(c * chunk, chunk), chunk)
        v = vb_ref[r, :]
        sc = _dot_nt(qb_ref[r, :], kb_ref[r, :]) * intra
        of_ref[r, :] = _dot(sc.astype(BF16), v) + _dot(qd_ref[r, :], s.astype(BF16))
        return s * cdec + _dot_tn(kd_ref[r, :], v)

    s_fin = lax.fori_loop(0, n_chunks, body, jnp.zeros((HEAD_DIM, HEAD_DIM), F32),
                          unroll=unroll)
    s_ref[0, 0] = s_fin
    o_ref[...] = _group_norm_gate(of_ref[...], gn_ref[0], g_ref[...]).astype(o_ref.dtype)


def _ret_prompt(proj, gn_w, batch, seq, h_ret):
    chunk = math.gcd(seq, RET_CHUNK)
    n_chunks = seq // chunk
    cos2, sin2 = _rotary_tables(jnp.arange(seq))
    lg = _log_decay(h_ret)
    i = jnp.arange(chunk, dtype=F32)
    rel = i[:, None] - i[None, :]
    intra = jnp.exp(jnp.where(rel[None] >= 0, rel[None] * lg[:, None, None], -jnp.inf))
    ones = jnp.ones((1, 1, HEAD_DIM), F32)
    qdec = jnp.exp((i + 1.0)[None, :, None] * lg[:, None, None]) * ones
    kdec = jnp.exp((chunk - 1.0 - i)[None, :, None] * lg[:, None, None]) * ones
    cdec = jnp.exp(chunk * lg)[:, None, None] * jnp.ones((1, 8, HEAD_DIM), F32)
    gn = gn_w.astype(F32).reshape(h_ret, 1, HEAD_DIM)
    hd = HEAD_DIM

    def col(group):
        return pl.BlockSpec((seq, hd), lambda b, h: (b, group * h_ret + h))

    const = lambda shape: pl.BlockSpec(shape, lambda b, h: (h, 0, 0))
    unroll = math.gcd(n_chunks, 16)
    return pl.pallas_call(
        functools.partial(_ret_prompt_kernel, chunk=chunk, n_chunks=n_chunks, unroll=unroll),
        grid=(batch, h_ret),
        in_specs=[col(0), col(1), col(2), col(3),
                  pl.BlockSpec((seq, hd), lambda b, h: (0, 0)),
                  pl.BlockSpec((seq, hd), lambda b, h: (0, 0)),
                  const((1, chunk, chunk)), const((1, chunk, hd)), const((1, chunk, hd)),
                  const((1, 8, hd)), const((1, 1, hd))],
        out_specs=[pl.BlockSpec((seq, hd), lambda b, h: (b, h)),
                   pl.BlockSpec((1, 1, hd, hd), lambda b, h: (b, h, 0, 0))],
        out_shape=[jax.ShapeDtypeStruct((batch * seq, h_ret * hd), BF16),
                   jax.ShapeDtypeStruct((batch, h_ret, hd, hd), F32)],
        scratch_shapes=[pltpu.VMEM((seq, hd), BF16)] * 5 + [pltpu.VMEM((seq, hd), F32)],
        compiler_params=_params(2 * 7 * seq * hd * 4 + 16 * MIB, 2),
        name="retention_prompt",
    )(proj, proj, proj, proj, cos2, sin2, intra, qdec, kdec, cdec, gn)


def _block_keep(gate_t, n_past, eye_bf16):
    blk_id = lax.broadcasted_iota(jnp.int32, gate_t.shape, 0)
    g = jnp.where(blk_id < n_past, gate_t, NEG_INF)
    rank = jnp.zeros(gate_t.shape, F32)
    for i in range(n_past):
        gi = g[i:i + 1, :]
        rank = rank + jnp.where(gi > g, 1.0,
                                jnp.where(gi == g, jnp.where(blk_id > i, 1.0, 0.0), 0.0))
    keep_t = jnp.where(blk_id < n_past, jnp.where(rank < float(MOBA_TOPK), 1.0, 0.0), 0.0)
    pad = jnp.zeros((HEAD_DIM - gate_t.shape[0], gate_t.shape[1]), F32)
    keep_t = jnp.concatenate([keep_t, pad], axis=0).astype(BF16)
    return _dot_tn(keep_t, eye_bf16)


def _moba_prompt_kernel(q_ref, k_ref, v_ref, o_ref, *, n_blocks):
    blk = MOBA_BLOCK
    scale = HEAD_DIM ** -0.5
    kb = k_ref[...]
    vb = v_ref[...]
    kmean = jnp.mean(kb.astype(F32).reshape(n_blocks, blk, HEAD_DIM), axis=1)
    n_rank_rows = -(-n_blocks // 8) * 8
    kmean_pad = jnp.concatenate(
        [kmean, jnp.zeros((HEAD_DIM - n_blocks, HEAD_DIM), F32)], axis=0)
    row = lax.broadcasted_iota(jnp.int32, (blk, blk), 0)
    colv = lax.broadcasted_iota(jnp.int32, (blk, blk), 1)
    causal = colv <= row
    r128 = lax.broadcasted_iota(jnp.int32, (HEAD_DIM, HEAD_DIM), 0)
    c128 = lax.broadcasted_iota(jnp.int32, (HEAD_DIM, HEAD_DIM), 1)
    eye_bf16 = jnp.where(r128 == c128, 1.0, 0.0).astype(BF16)

    for t in range(n_blocks):
        rows = slice(t * blk, (t + 1) * blk)
        qf = q_ref[rows, :]
        s = _dot_nt((qf * scale).astype(BF16), kb[:(t + 1) * blk])
        pieces = []
        if t > MOBA_TOPK:
            gate_t = _dot_nt(kmean_pad, qf, precision=lax.Precision.HIGHEST)
            keep = _block_keep(gate_t[:n_rank_rows], t, eye_bf16)
        for j in range(t):
            sj = s[:, j * blk:(j + 1) * blk]
            if t > MOBA_TOPK:
                sj = jnp.where(keep[:, j:j + 1] > 0.5, sj, NEG_INF)
            pieces.append(sj)
        pieces.append(jnp.where(causal, s[:, t * blk:], NEG_INF))
        s = jnp.concatenate(pieces, axis=1) if len(pieces) > 1 else pieces[0]
        m = jnp.max(s, axis=1, keepdims=True)
        p = jnp.exp(s - m)
        l = jnp.sum(p, axis=1, keepdims=True)
        o = _dot(p.astype(BF16), vb[:(t + 1) * blk]) / l
        o_ref[rows, :] = o.astype(o_ref.dtype)


def _moba_prompt(proj, k2d, v2d, batch, seq, h_att, q_col0):
    assert seq % MOBA_BLOCK == 0
    n_blocks = seq // MOBA_BLOCK
    assert n_blocks <= HEAD_DIM
    hd = HEAD_DIM
    head = pl.BlockSpec((seq, hd), lambda b, h: (b, h))
    return pl.pallas_call(
        functools.partial(_moba_prompt_kernel, n_blocks=n_blocks),
        grid=(batch, h_att),
        in_specs=[pl.BlockSpec((seq, hd), lambda b, h: (b, q_col0 + h)), head, head],
        out_specs=head,
        out_shape=jax.ShapeDtypeStruct((batch * seq, h_att * hd), BF16),
        compiler_params=_params(32 * MIB, 2),
        name="moba_prompt",
    )(proj, k2d, v2d)


def _row_to_col(x_row, eye):
    n = x_row.shape[1]
    return jnp.sum(jnp.where(eye, jnp.broadcast_to(x_row, (n, n)), 0.0), axis=1, keepdims=True)


def _ret_sample_kernel(q_ref, k_ref, v_ref, g_ref, s_ref, cos_ref, sin_ref, dec_ref, gn_ref,
                       o_ref, so_ref, *, h_ret):
    half = HEAD_DIM // 2
    scale = HEAD_DIM ** -0.5
    cos = cos_ref[...]
    sin = sin_ref[...]
    q = q_ref[0]
    q = q * cos + pltpu.roll(q, half, 1) * sin
    k = k_ref[0]
    k = (k * cos + pltpu.roll(k, half, 1) * sin) * scale
    v = v_ref[0]
    r0 = lax.broadcasted_iota(jnp.int32, (HEAD_DIM, HEAD_DIM), 0)
    r1 = lax.broadcasted_iota(jnp.int32, (HEAD_DIM, HEAD_DIM), 1)
    eye = r0 == r1
    for h in range(h_ret):
        hh = slice(h, h + 1)
        kcol = _row_to_col(k[hh, :], eye)
        qcol = _row_to_col(q[hh, :], eye)
        s_new = s_ref[0, 0, h] * dec_ref[hh, :] + kcol * v[hh, :]
        so_ref[0, 0, h] = s_new
        o = jnp.sum(qcol * s_new, axis=0, keepdims=True)
        o_ref[0, hh, :] = _group_norm_gate(o, gn_ref[hh, :], g_ref[0, hh, :]).astype(o_ref.dtype)


def _ret_sample(proj3, state, layer, gn_w, pos, dec_batch, h_ret):
    hd = HEAD_DIM
    cos2, sin2 = _rotary_tables(pos)
    dec = jnp.exp(1.0 * _log_decay(h_ret))[:, None] * jnp.ones((1, hd), F32)
    gn = gn_w.astype(F32).reshape(h_ret, hd)
    grp = lambda g: pl.BlockSpec((1, h_ret, hd), lambda b: (b, g, 0))
    full = lambda shape: pl.BlockSpec(shape, lambda b: (0,) * len(shape))
    return pl.pallas_call(
        functools.partial(_ret_sample_kernel, h_ret=h_ret),
        grid=(dec_batch,),
        in_specs=[grp(0), grp(1), grp(2), grp(3),
                  pl.BlockSpec((1, 1, h_ret, hd, hd), lambda b: (layer, b, 0, 0, 0)),
                  full((1, hd)), full((1, hd)), full((h_ret, hd)), full((h_ret, hd))],
        out_specs=[pl.BlockSpec((1, h_ret, hd), lambda b: (b, 0, 0)),
                   pl.BlockSpec((1, 1, h_ret, hd, hd), lambda b: (0, b, 0, 0, 0))],
        out_shape=[jax.ShapeDtypeStruct((dec_batch, h_ret, hd), F32),
                   jax.ShapeDtypeStruct((1, dec_batch, h_ret, hd, hd), F32)],
        compiler_params=_params(16 * MIB, 1),
        name="retention_sample",
    )(proj3, proj3, proj3, proj3, state, cos2, sin2, dec, gn)


def _kmean_kernel(pt_ref, *refs, pages_per_block):
    del pt_ref
    o_ref = refs[-1]
    pages = refs[:-1]
    n_out = len(pages) // pages_per_block
    n_heads = pages[0].shape[3]
    inv_n = 1.0 / (pages[0].shape[2] * pages_per_block)
    for i in range(n_out):
        acc = jnp.sum(pages[i * pages_per_block][0, 0], axis=0)
        for p in range(1, pages_per_block):
            acc = acc + jnp.sum(pages[i * pages_per_block + p][0, 0], axis=0)
        acc = acc * inv_n
        for h in range(n_heads):
            o_ref[0, h, i:i + 1, :] = acc[h:h + 1, :]


def _kmean_sample(cache, layer, page_table, pages_per_block):
    db, n_pages = page_table.shape
    page, n_heads, hd = cache.shape[2:]
    pps = KMEAN_PAGES_PER_STEP
    assert n_pages % pps == 0 and pps % pages_per_block == 0
    blocks_per_step = pps // pages_per_block
    assert blocks_per_step % 8 == 0

    def page_spec(i):
        return pl.BlockSpec((1, 1, page, n_heads, hd),
                            lambda b, c, pt: (layer, pt[b, c * pps + i], 0, 0, 0))

    grid_spec = pltpu.PrefetchScalarGridSpec(
        num_scalar_prefetch=1,
        grid=(db, n_pages // pps),
        in_specs=[page_spec(i) for i in range(pps)],
        out_specs=pl.BlockSpec((1, n_heads, blocks_per_step, hd), lambda b, c, pt: (b, 0, c, 0)),
    )
    return pl.pallas_call(
        functools.partial(_kmean_kernel, pages_per_block=pages_per_block),
        grid_spec=grid_spec,
        out_shape=jax.ShapeDtypeStruct((db, n_heads, n_pages // pages_per_block, hd), F32),
        compiler_params=_params(2 * pps * page * n_heads * hd * 4 + 8 * MIB, 2),
        name="moba_sample_kmean",
    )(page_table, *([cache] * pps))


def _select_kernel(km_ref, q_ref, o_ref, *, n_sel):
    n_heads, nb = km_ref.shape[1], km_ref.shape[2]
    blk_id = lax.broadcasted_iota(jnp.int32, (nb, 1), 0).astype(F32)
    row = lax.broadcasted_iota(jnp.int32, o_ref.shape[1:], 0)
    lane = lax.broadcasted_iota(jnp.int32, o_ref.shape[1:], 1)
    out = jnp.zeros(o_ref.shape[1:], jnp.int32)
    for h in range(n_heads):
        gate = jnp.sum(km_ref[0, h] * q_ref[0, h:h + 1, :], axis=1, keepdims=True)
        for r in range(n_sel):
            best = jnp.max(gate, axis=0, keepdims=True)
            idx = jnp.min(jnp.where(gate == best, blk_id, float(nb)), axis=0, keepdims=True)
            out = jnp.where((row == r) & (lane == h), idx.astype(jnp.int32), out)
            gate = jnp.where(blk_id == idx, NEG_INF, gate)
    o_ref[0] = out


def _select_sample(kmean, proj3, q_group, n_sel):
    db, n_heads, nb, hd = kmean.shape
    assert n_sel <= 8 and n_heads <= hd
    return pl.pallas_call(
        functools.partial(_select_kernel, n_sel=n_sel),
        grid=(db,),
        in_specs=[pl.BlockSpec((1, n_heads, nb, hd), lambda b: (b, 0, 0, 0)),
                  pl.BlockSpec((1, n_heads, hd), lambda b: (b, q_group, 0))],
        out_specs=pl.BlockSpec((1, 8, hd), lambda b: (b, 0, 0)),
        out_shape=jax.ShapeDtypeStruct((db, 8, hd), jnp.int32),
        compiler_params=_params(16 * MIB, 1),
        name="moba_sample_select",
    )(kmean, proj3)


def _attn_sample_kernel(pt_ref, sel_ref, q_ref, kn_ref, vn_ref, *refs, n_pages_sel, head_tile):
    del pt_ref, sel_ref
    o_ref = refs[-1]
    k_pages = refs[:n_pages_sel]
    v_pages = refs[n_pages_sel:2 * n_pages_sel]
    hd = HEAD_DIM
    scale = hd ** -0.5
    page = k_pages[0].shape[2]
    rows = page * head_tile
    k_all = jnp.concatenate([kp[0, 0].reshape(rows, hd).astype(BF16) for kp in k_pages], axis=0)
    v_all = jnp.concatenate([vp[0, 0].reshape(rows, hd).astype(BF16) for vp in v_pages], axis=0)
    q = q_ref[0]
    q8 = jnp.broadcast_to(q * scale, (8, hd)).astype(BF16)
    s = _dot_nt(q8, k_all)
    head_in_tile = pl.program_id(1) % head_tile
    col = lax.broadcasted_iota(jnp.int32, s.shape, 1)
    s = jnp.where(col % head_tile == head_in_tile, s, NEG_INF)
    s_own = jnp.sum(kn_ref[0] * q, axis=1, keepdims=True) * scale
    m = jnp.maximum(jnp.max(s, axis=1, keepdims=True), s_own)
    e = jnp.exp(s - m)
    e_own = jnp.exp(s_own - m)
    l = jnp.sum(e, axis=1, keepdims=True) + e_own
    o = _dot(e.astype(BF16), v_all) + e_own * vn_ref[0]
    o_ref[0] = (o / l)[0:1, :]


def _attn_sample(cache_k, cache_v, layer, page_table, sel2, proj4, groups, dec_batch,
                 n_sel, pages_per_block):
    page, h_att, hd = cache_k.shape[2:]
    head_tile = 8
    assert h_att % head_tile == 0
    n_groups, q_group = groups
    n_pages_sel = n_sel * pages_per_block

    def vec_spec(group):
        return pl.BlockSpec(
            (1, 1, hd), lambda b, h, pt, sel: ((b * n_groups + group) * h_att + h, 0, 0))

    def page_spec(r, i):
        return pl.BlockSpec(
            (1, 1, page, head_tile, hd),
            lambda b, h, pt, sel: (layer, pt[b, sel[b * 8 + r, h] * pages_per_block + i], 0,
                                   h // head_tile, 0))

    page_specs = [page_spec(r, i) for r in range(n_sel) for i in range(pages_per_block)]
    grid_spec = pltpu.PrefetchScalarGridSpec(
        num_scalar_prefetch=2,
        grid=(dec_batch, h_att),
        in_specs=[vec_spec(q_group), vec_spec(q_group + 1), vec_spec(q_group + 2)]
        + page_specs + page_specs,
        out_specs=pl.BlockSpec((1, 1, hd), lambda b, h, pt, sel: (b * h_att + h, 0, 0)),
    )
    return pl.pallas_call(
        functools.partial(_attn_sample_kernel, n_pages_sel=n_pages_sel, head_tile=head_tile),
        grid_spec=grid_spec,
        out_shape=jax.ShapeDtypeStruct((dec_batch * h_att, 1, hd), F32),
        compiler_params=_params(
            4 * n_pages_sel * page * head_tile * hd * 4 + 16 * MIB, 2),
        name="moba_sample_attn",
    )(page_table, sel2, proj4, proj4, proj4,
      *([cache_k] * n_pages_sel), *([cache_v] * n_pages_sel))


def _pad_rows(x, rows):
    return jnp.pad(x, ((0, rows - x.shape[0]), (0, 0)))


def kernel(x_prompt, x_sample, cache_k, cache_v, state_ret, page_table, w_in, w_out, norm_mix,
           ret_gn, norm_ffn, w_gate, w_up, w_down, norm_final):
    batch, seq, d_model = x_prompt.shape
    dec_batch, dec_seq, _ = x_sample.shape
    depth, n_pool, page_size, h_att, hd = cache_k.shape
    h_ret = state_ret.shape[2]
    w_ret, w_att = h_ret * hd, h_att * hd
    n_pages = page_table.shape[1]
    past_len = n_pages * page_size
    assert hd == HEAD_DIM and h_ret == h_att and dec_seq == 1
    assert MOBA_BLOCK % page_size == 0 and past_len % MOBA_BLOCK == 0
    pages_per_block = MOBA_BLOCK // page_size
    n_past_blocks = past_len // MOBA_BLOCK
    assert n_past_blocks >= MOBA_TOPK and dec_batch <= SAMPLE_ROWS
    n_groups = 7
    att_group = 4
    in_cols = w_in.shape[2]
    assert in_cols == n_groups * w_ret

    m_p = batch * seq
    yp = x_prompt.reshape(m_p, d_model)
    ys = _pad_rows(x_sample.reshape(dec_batch, d_model), SAMPLE_ROWS)
    pos_s = past_len + jnp.arange(dec_seq)

    w_in_b, w_out_b = w_in.astype(BF16), w_out.astype(BF16)
    w_gate_b, w_up_b, w_down_b = w_gate.astype(BF16), w_up.astype(BF16), w_down.astype(BF16)
    d_ff = w_gate.shape[2]
    tn_ff = 256
    assert d_ff % tn_ff == 0

    rp_states, ks_rows, vs_rows, rs_states = [], [], [], []
    rows5 = (depth, batch, seq, h_att, hd)
    k_prompt = v_prompt = None
    for l in range(depth):
        hp = _rmsnorm(yp, norm_mix[l], BF16, 256)
        proj_p = _proj(hp, w_in_b[l], F32, 1024, 1024, n_cols=5 * w_ret)
        k_prompt, k2d = _kv_proj(hp, w_in_b[l], 5 * w_ret, l, rows5, k_prompt, 1024)
        v_prompt, v2d = _kv_proj(hp, w_in_b[l], 6 * w_ret, l, rows5, v_prompt, 1024)
        ret_o, ret_s = _ret_prompt(proj_p, ret_gn[l], batch, seq, h_ret)
        att_o = _moba_prompt(proj_p, k2d, v2d, batch, seq, h_att, att_group * h_ret)
        rp_states.append(ret_s)
        yp = _outproj(ret_o, att_o, w_out_b[l], yp, 1024, 1024)
        hf = _rmsnorm(yp, norm_ffn[l], BF16, 256)
        act = _gateup(hf, w_gate_b[l], w_up_b[l], 1024, tn_ff)
        yp = _down(act, w_down_b[l], yp, 512, 512)

        hs = _rmsnorm(ys, norm_mix[l], BF16, SAMPLE_ROWS)
        proj_s = _proj(hs, w_in_b[l], F32, SAMPLE_ROWS, 1024)
        proj3 = proj_s.reshape(SAMPLE_ROWS, n_groups * h_ret, hd)
        proj4 = proj_s.reshape(SAMPLE_ROWS * n_groups * h_ret, 1, hd)
        ret_os, ret_ss = _ret_sample(proj3, state_ret, l, ret_gn[l], pos_s, dec_batch, h_ret)
        kmean = _kmean_sample(cache_k, l, page_table, pages_per_block)
        sel = _select_sample(kmean, proj3, att_group, MOBA_TOPK)
        att_os = _attn_sample(cache_k, cache_v, l, page_table, sel.reshape(dec_batch * 8, hd),
                              proj4, (n_groups, att_group), dec_batch, MOBA_TOPK,
                              pages_per_block)
        ks_rows.append(proj_s[:dec_batch, 5 * w_ret:6 * w_ret].reshape(dec_batch, 1, h_att, hd))
        vs_rows.append(proj_s[:dec_batch, 6 * w_ret:7 * w_ret].reshape(dec_batch, 1, h_att, hd))
        rs_states.append(ret_ss[0])
        mix_r = _pad_rows(ret_os.reshape(dec_batch, w_ret), SAMPLE_ROWS).astype(BF16)
        mix_a = _pad_rows(att_os.reshape(dec_batch, w_att), SAMPLE_ROWS).astype(BF16)
        ys = _outproj(mix_r, mix_a, w_out_b[l], ys, SAMPLE_ROWS, 1024)
        hfs = _rmsnorm(ys, norm_ffn[l], BF16, SAMPLE_ROWS)
        acts = _gateup(hfs, w_gate_b[l], w_up_b[l], SAMPLE_ROWS, tn_ff)
        ys = _down(acts, w_down_b[l], ys, SAMPLE_ROWS, 512)

    y_prompt = _rmsnorm(yp, norm_final, F32, 256).reshape(batch, seq, d_model)
    y_sample = _rmsnorm(ys, norm_final, F32, SAMPLE_ROWS)[:dec_batch].reshape(dec_batch, 1, d_model)
    return (y_prompt, y_sample, k_prompt, v_prompt, jnp.stack(rp_states),
            jnp.stack(ks_rows), jnp.stack(vs_rows), jnp.stack(rs_states))
```

```python
import functools
import math

import jax
import jax.numpy as jnp
from jax import lax
from jax.experimental import pallas as pl
from jax.experimental.pallas import tpu as pltpu

HEAD_DIM = 128
RET_CHUNK = 128
MOBA_BLOCK = 256
MOBA_TOPK = 3
ROPE_BASE = 10000.0
NORM_EPS = 1e-6
GN_EPS = 1e-5

BF16 = jnp.bfloat16
F32 = jnp.float32
NEG_INF = float("-inf")
MIB = 1024 * 1024
V7X_VMEM_BUDGET = 56 * MIB
SAMPLE_ROWS = 16
KMEAN_PAGES_PER_STEP = 16


def _params(vmem_bytes, n_axes):
    return pltpu.CompilerParams(
        dimension_semantics=("arbitrary",) * n_axes,
        vmem_limit_bytes=int(min(max(vmem_bytes, 16 * MIB), V7X_VMEM_BUDGET)))


def _dot(a, b):
    return jnp.dot(a, b, preferred_element_type=F32)


def _dot_nt(a, b, precision=None):
    return lax.dot_general(a, b, (((1,), (1,)), ((), ())), precision=precision,
                           preferred_element_type=F32)


def _dot_tn(a, b):
    return lax.dot_general(a, b, (((0,), (0,)), ((), ())), preferred_element_type=F32)


def _silu(x):
    return x * (1.0 / (1.0 + jnp.exp(-x)))


def _rmsnorm_kernel(x_ref, g_ref, o_ref):
    x = x_ref[...]
    y = x * lax.rsqrt(jnp.mean(x * x, axis=-1, keepdims=True) + NORM_EPS)
    o_ref[...] = (y * g_ref[...]).astype(o_ref.dtype)


def _rmsnorm(x, g, out_dtype, tm):
    m, d = x.shape
    return pl.pallas_call(
        _rmsnorm_kernel,
        grid=(m // tm,),
        in_specs=[pl.BlockSpec((tm, d), lambda i: (i, 0)),
                  pl.BlockSpec((1, d), lambda i: (0, 0))],
        out_specs=pl.BlockSpec((tm, d), lambda i: (i, 0)),
        out_shape=jax.ShapeDtypeStruct((m, d), out_dtype),
        compiler_params=_params(6 * tm * d * 4, 1),
        name="rmsnorm",
    )(x, g.reshape(1, d))


def _first_m_step():
    return pl.program_id(1) == 0


def _weight_spec(k, tn, layer, col_block0=0, single_buffer=False):
    mode = pl.Buffered(1) if single_buffer else None
    return pl.BlockSpec((None, k, tn), lambda n, m: (layer, 0, col_block0 + n), pipeline_mode=mode)


def _proj_kernel(a_ref, as_ref, w_ref, o_ref, os_ref, wb_ref):
    @pl.when(_first_m_step())
    def _():
        wb_ref[...] = w_ref[...].astype(BF16)
        os_ref[...] = _dot(as_ref[...], wb_ref[...])

    o_ref[...] = _dot(a_ref[...], wb_ref[...])


def _proj(a, a_s, w, layer, n_cols, tm, tn):
    m, k = a.shape
    rows_s = a_s.shape[0]
    assert m % tm == 0 and n_cols % tn == 0
    return pl.pallas_call(
        _proj_kernel,
        grid=(n_cols // tn, m // tm),
        in_specs=[pl.BlockSpec((tm, k), lambda n, m: (m, 0)),
                  pl.BlockSpec((rows_s, k), lambda n, m: (0, 0)),
                  _weight_spec(k, tn, layer)],
        out_specs=[pl.BlockSpec((tm, tn), lambda n, m: (m, n)),
                   pl.BlockSpec((rows_s, tn), lambda n, m: (0, n))],
        out_shape=[jax.ShapeDtypeStruct((m, n_cols), F32),
                   jax.ShapeDtypeStruct((rows_s, n_cols), F32)],
        scratch_shapes=[pltpu.VMEM((k, tn), BF16)],
        compiler_params=_params(2 * (tm * k * 2 + k * tn * 4 + tm * tn * 4) + k * tn * 2 + 8 * MIB, 2),
        name="in_proj",
    )(a, a_s, w)


def _kv_proj_kernel(*refs):
    a_ref, as_ref, w_ref = refs[:3]
    o5_ref, o2_ref, os_ref, wb_ref = refs[-4:]

    @pl.when(_first_m_step())
    def _():
        wb_ref[...] = w_ref[...].astype(BF16)
        os_ref[...] = _dot(as_ref[...], wb_ref[...])

    acc = _dot(a_ref[...], wb_ref[...])
    o2_ref[...] = acc.astype(o2_ref.dtype)
    for h in range(o5_ref.shape[3]):
        o5_ref[0, 0, :, h, :] = acc[:, h * HEAD_DIM:(h + 1) * HEAD_DIM]


def _kv_proj(a, a_s, w, layer, col0, rows5, prev, tm):
    m, k = a.shape
    rows_s = a_s.shape[0]
    depth, batch, seq, n_heads, hd = rows5
    heads_per_step = 8
    tn = heads_per_step * hd
    assert seq % tm == 0 and n_heads % heads_per_step == 0 and col0 % tn == 0
    tiles_per_seq = seq // tm
    in_specs = [pl.BlockSpec((tm, k), lambda n, m: (m, 0)),
                pl.BlockSpec((rows_s, k), lambda n, m: (0, 0)),
                _weight_spec(k, tn, layer, col0 // tn, single_buffer=True)]
    args = [a, a_s, w]
    aliases = {}
    if prev is not None:
        in_specs.append(pl.BlockSpec(memory_space=pl.ANY))
        args.append(prev)
        aliases = {3: 0}
    return pl.pallas_call(
        _kv_proj_kernel,
        grid=(n_heads // heads_per_step, m // tm),
        in_specs=in_specs,
        out_specs=[pl.BlockSpec((1, 1, tm, heads_per_step, hd),
                                lambda n, m: (layer, m // tiles_per_seq, m % tiles_per_seq, n, 0)),
                   pl.BlockSpec((tm, tn), lambda n, m: (m, n)),
                   pl.BlockSpec((rows_s, tn), lambda n, m: (0, n))],
        out_shape=[jax.ShapeDtypeStruct(rows5, F32),
                   jax.ShapeDtypeStruct((m, n_heads * hd), BF16),
                   jax.ShapeDtypeStruct((rows_s, n_heads * hd), F32)],
        scratch_shapes=[pltpu.VMEM((k, tn), BF16)],
        input_output_aliases=aliases,
        compiler_params=_params(
            2 * (tm * k * 2 + tm * tn * 6) + k * tn * 6 + 8 * MIB, 2),
        name="kv_proj",
    )(*args)


def _outproj_kernel(a1_ref, a2_ref, a1s_ref, a2s_ref, w_ref, r_ref, rs_ref, o_ref, os_ref, wb_ref):
    k1 = a1_ref.shape[1]

    def mix(x1_ref, x2_ref, res_ref):
        return res_ref[...] + (_dot(x1_ref[...], wb_ref[:k1, :]) + _dot(x2_ref[...], wb_ref[k1:, :]))

    @pl.when(_first_m_step())
    def _():
        wb_ref[...] = w_ref[...].astype(BF16)
        os_ref[...] = mix(a1s_ref, a2s_ref, rs_ref)

    o_ref[...] = mix(a1_ref, a2_ref, r_ref)


def _outproj(a1, a2, a1_s, a2_s, w, layer, res, res_s, tm, tn):
    m, k1 = a1.shape
    k2 = a2.shape[1]
    rows_s = a1_s.shape[0]
    k, n = k1 + k2, w.shape[2]
    assert m % tm == 0 and n % tn == 0
    return pl.pallas_call(
        _outproj_kernel,
        grid=(n // tn, m // tm),
        in_specs=[pl.BlockSpec((tm, k1), lambda n, m: (m, 0)),
                  pl.BlockSpec((tm, k2), lambda n, m: (m, 0)),
                  pl.BlockSpec((rows_s, k1), lambda n, m: (0, 0)),
                  pl.BlockSpec((rows_s, k2), lambda n, m: (0, 0)),
                  _weight_spec(k, tn, layer),
                  pl.BlockSpec((tm, tn), lambda n, m: (m, n)),
                  pl.BlockSpec((rows_s, tn), lambda n, m: (0, n))],
        out_specs=[pl.BlockSpec((tm, tn), lambda n, m: (m, n)),
                   pl.BlockSpec((rows_s, tn), lambda n, m: (0, n))],
        out_shape=[jax.ShapeDtypeStruct((m, n), F32),
                   jax.ShapeDtypeStruct((rows_s, n), F32)],
        scratch_shapes=[pltpu.VMEM((k, tn), BF16)],
        compiler_params=_params(
            2 * (tm * k * 2 + k * tn * 4 + 2 * tm * tn * 4) + k * tn * 2 + 8 * MIB, 2),
        name="out_proj",
    )(a1, a2, a1_s, a2_s, w, res, res_s)


def _gateup_kernel(h_ref, hs_ref, wg_ref, wu_ref, o_ref, os_ref, wgb_ref, wub_ref):
    def swiglu_in(x_ref):
        x = x_ref[...]
        return _silu(_dot(x, wgb_ref[...])) * _dot(x, wub_ref[...])

    @pl.when(_first_m_step())
    def _():
        wgb_ref[...] = wg_ref[...].astype(BF16)
        wub_ref[...] = wu_ref[...].astype(BF16)
        os_ref[...] = swiglu_in(hs_ref).astype(os_ref.dtype)

    o_ref[...] = swiglu_in(h_ref).astype(o_ref.dtype)


def _gateup(h, h_s, wg, wu, layer, tm, tn):
    m, k = h.shape
    rows_s = h_s.shape[0]
    n = wg.shape[2]
    assert m % tm == 0 and n % tn == 0
    return pl.pallas_call(
        _gateup_kernel,
        grid=(n // tn, m // tm),
        in_specs=[pl.BlockSpec((tm, k), lambda n, m: (m, 0)),
                  pl.BlockSpec((rows_s, k), lambda n, m: (0, 0)),
                  _weight_spec(k, tn, layer), _weight_spec(k, tn, layer)],
        out_specs=[pl.BlockSpec((tm, tn), lambda n, m: (m, n)),
                   pl.BlockSpec((rows_s, tn), lambda n, m: (0, n))],
        out_shape=[jax.ShapeDtypeStruct((m, n), BF16),
                   jax.ShapeDtypeStruct((rows_s, n), BF16)],
        scratch_shapes=[pltpu.VMEM((k, tn), BF16), pltpu.VMEM((k, tn), BF16)],
        compiler_params=_params(
            2 * (tm * k * 2 + 2 * k * tn * 4 + tm * tn * 2) + 2 * k * tn * 2 + 8 * MIB, 2),
        name="ffn_gate_up",
    )(h, h_s, wg, wu)


def _down_kernel(a_ref, as_ref, w_ref, r_ref, rs_ref, o_ref, os_ref):
    @pl.when(_first_m_step())
    def _():
        os_ref[...] = rs_ref[...] + _dot(as_ref[...], w_ref[...])

    o_ref[...] = r_ref[...] + _dot(a_ref[...], w_ref[...])


def _down(a, a_s, w_bf16, layer, res, res_s, tm, tn):
    m, k = a.shape
    rows_s = a_s.shape[0]
    n = w_bf16.shape[2]
    assert m % tm == 0 and n % tn == 0
    return pl.pallas_call(
        _down_kernel,
        grid=(n // tn, m // tm),
        in_specs=[pl.BlockSpec((tm, k), lambda n, m: (m, 0)),
                  pl.BlockSpec((rows_s, k), lambda n, m: (0, 0)),
                  _weight_spec(k, tn, layer),
                  pl.BlockSpec((tm, tn), lambda n, m: (m, n)),
                  pl.BlockSpec((rows_s, tn), lambda n, m: (0, n))],
        out_specs=[pl.BlockSpec((tm, tn), lambda n, m: (m, n)),
                   pl.BlockSpec((rows_s, tn), lambda n, m: (0, n))],
        out_shape=[jax.ShapeDtypeStruct((m, n), F32),
                   jax.ShapeDtypeStruct((rows_s, n), F32)],
        compiler_params=_params(2 * (tm * k * 2 + k * tn * 2 + 2 * tm * tn * 4) + 8 * MIB, 2),
        name="ffn_down",
    )(a, a_s, w_bf16, res, res_s)


def _rotary_tables(pos):
    half = HEAD_DIM // 2
    inv = ROPE_BASE ** (-jnp.arange(half, dtype=F32) / half)
    ang = pos.astype(F32)[:, None] * inv[None, :]
    cos, sin = jnp.cos(ang), jnp.sin(ang)
    return (jnp.concatenate([cos, cos], axis=-1), jnp.concatenate([-sin, sin], axis=-1))


def _log_decay(h_ret):
    return jnp.log1p(-jnp.exp2(-5.0 - jnp.arange(h_ret, dtype=F32)))


def _group_norm_gate(o, gn, g):
    mu = jnp.mean(o, axis=-1, keepdims=True)
    d = o - mu
    var = jnp.mean(d * d, axis=-1, keepdims=True)
    return _silu(g) * (d * lax.rsqrt(var + GN_EPS) * gn)


def _ret_prompt_kernel(q_ref, k_ref, v_ref, g_ref, cos_ref, sin_ref, intra_ref, qdec_ref,
                       kdec_ref, cdec_ref, gn_ref, o_ref, s_ref,
                       qb_ref, kb_ref, qd_ref, kd_ref, vb_ref, of_ref, *, chunk, n_chunks, unroll):
    half = HEAD_DIM // 2
    scale = HEAD_DIM ** -0.5
    seq = q_ref.shape[0]
    intra = intra_ref[0]
    cdec = cdec_ref[0, 0:1, :]

    cos = cos_ref[...]
    sin = sin_ref[...]
    q = q_ref[...]
    q = q * cos + pltpu.roll(q, half, 1) * sin
    k = k_ref[...]
    k = (k * cos + pltpu.roll(k, half, 1) * sin) * scale
    qb_ref[...] = q.astype(BF16)
    kb_ref[...] = k.astype(BF16)
    qd = q.reshape(n_chunks, chunk, HEAD_DIM) * qdec_ref[...]
    kd = k.reshape(n_chunks, chunk, HEAD_DIM) * kdec_ref[...]
    qd_ref[...] = qd.reshape(seq, HEAD_DIM).astype(BF16)
    kd_ref[...] = kd.reshape(seq, HEAD_DIM).astype(BF16)
    vb_ref[...] = v_ref[...].astype(BF16)

    def body(c, s):
        r = pl.ds(pl.multiple_of(c * chunk, chunk), chunk)
        v = vb_ref[r, :]
        sc = _dot_nt(qb_ref[r, :], kb_ref[r, :]) * intra
        of_ref[r, :] = _dot(sc.astype(BF16), v) + _dot(qd_ref[r, :], s.astype(BF16))
        return s * cdec + _dot_tn(kd_ref[r, :], v)

    s_fin = lax.fori_loop(0, n_chunks, body, jnp.zeros((HEAD_DIM, HEAD_DIM), F32),
                          unroll=unroll)
    s_ref[0, 0] = s_fin
    o_ref[...] = _group_norm_gate(of_ref[...], gn_ref[0], g_ref[...]).astype(o_ref.dtype)


def _ret_prompt(proj, gn_w, batch, seq, h_ret):
    chunk = math.gcd(seq, RET_CHUNK)
    n_chunks = seq // chunk
    cos2, sin2 = _rotary_tables(jnp.arange(seq))
    lg = _log_decay(h_ret)
    i = jnp.arange(chunk, dtype=F32)
    rel = i[:, None] - i[None, :]
    intra = jnp.exp(jnp.where(rel[None] >= 0, rel[None] * lg[:, None, None], -jnp.inf))
    ones = jnp.ones((1, 1, HEAD_DIM), F32)
    qdec = jnp.exp((i + 1.0)[None, :, None] * lg[:, None, None]) * ones
    kdec = jnp.exp((chunk - 1.0 - i)[None, :, None] * lg[:, None, None]) * ones
    cdec = jnp.exp(chunk * lg)[:, None, None] * jnp.ones((1, 8, HEAD_DIM), F32)
    gn = gn_w.astype(F32).reshape(h_ret, 1, HEAD_DIM)
    hd = HEAD_DIM

    def col(group):
        return pl.BlockSpec((seq, hd), lambda b, h: (b, group * h_ret + h))

    const = lambda shape: pl.BlockSpec(shape, lambda b, h: (h, 0, 0))
    unroll = math.gcd(n_chunks, 16)
    return pl.pallas_call(
        functools.partial(_ret_prompt_kernel, chunk=chunk, n_chunks=n_chunks, unroll=unroll),
        grid=(batch, h_ret),
        in_specs=[col(0), col(1), col(2), col(3),
                  pl.BlockSpec((seq, hd), lambda b, h: (0, 0)),
                  pl.BlockSpec((seq, hd), lambda b, h: (0, 0)),
                  const((1, chunk, chunk)), const((1, chunk, hd)), const((1, chunk, hd)),
                  const((1, 8, hd)), const((1, 1, hd))],
        out_specs=[pl.BlockSpec((seq, hd), lambda b, h: (b, h)),
                   pl.BlockSpec((1, 1, hd, hd), lambda b, h: (b, h, 0, 0))],
        out_shape=[jax.ShapeDtypeStruct((batch * seq, h_ret * hd), BF16),
                   jax.ShapeDtypeStruct((batch, h_ret, hd, hd), F32)],
        scratch_shapes=[pltpu.VMEM((seq, hd), BF16)] * 5 + [pltpu.VMEM((seq, hd), F32)],
        compiler_params=_params(2 * 7 * seq * hd * 4 + 16 * MIB, 2),
        name="retention_prompt",
    )(proj, proj, proj, proj, cos2, sin2, intra, qdec, kdec, cdec, gn)


def _block_keep(gate_t, n_past, eye_bf16):
    blk_id = lax.broadcasted_iota(jnp.int32, gate_t.shape, 0)
    g = jnp.where(blk_id < n_past, gate_t, NEG_INF)
    rank = jnp.zeros(gate_t.shape, F32)
    for i in range(n_past):
        gi = g[i:i + 1, :]
        rank = rank + jnp.where(gi > g, 1.0,
                                jnp.where(gi == g, jnp.where(blk_id > i, 1.0, 0.0), 0.0))
    keep_t = jnp.where(blk_id < n_past, jnp.where(rank < float(MOBA_TOPK), 1.0, 0.0), 0.0)
    pad = jnp.zeros((HEAD_DIM - gate_t.shape[0], gate_t.shape[1]), F32)
    keep_t = jnp.concatenate([keep_t, pad], axis=0).astype(BF16)
    return _dot_tn(keep_t, eye_bf16)


def _moba_prompt_kernel(q_ref, k_ref, v_ref, o_ref, *, n_blocks):
    blk = MOBA_BLOCK
    scale = HEAD_DIM ** -0.5
    kb = k_ref[...]
    vb = v_ref[...]
    kmean = jnp.mean(kb.astype(F32).reshape(n_blocks, blk, HEAD_DIM), axis=1)
    n_rank_rows = -(-n_blocks // 8) * 8
    kmean_pad = jnp.concatenate(
        [kmean, jnp.zeros((HEAD_DIM - n_blocks, HEAD_DIM), F32)], axis=0)
    row = lax.broadcasted_iota(jnp.int32, (blk, blk), 0)
    colv = lax.broadcasted_iota(jnp.int32, (blk, blk), 1)
    causal = colv <= row
    r128 = lax.broadcasted_iota(jnp.int32, (HEAD_DIM, HEAD_DIM), 0)
    c128 = lax.broadcasted_iota(jnp.int32, (HEAD_DIM, HEAD_DIM), 1)
    eye_bf16 = jnp.where(r128 == c128, 1.0, 0.0).astype(BF16)

    for t in range(n_blocks):
        rows = slice(t * blk, (t + 1) * blk)
        qf = q_ref[rows, :]
        s = _dot_nt((qf * scale).astype(BF16), kb[:(t + 1) * blk])
        pieces = []
        if t > MOBA_TOPK:
            gate_t = _dot_nt(kmean_pad, qf, precision=lax.Precision.HIGHEST)
            keep = _block_keep(gate_t[:n_rank_rows], t, eye_bf16)
        for j in range(t):
            sj = s[:, j * blk:(j + 1) * blk]
            if t > MOBA_TOPK:
                sj = jnp.where(keep[:, j:j + 1] > 0.5, sj, NEG_INF)
            pieces.append(sj)
        pieces.append(jnp.where(causal, s[:, t * blk:], NEG_INF))
        s = jnp.concatenate(pieces, axis=1) if len(pieces) > 1 else pieces[0]
        m = jnp.max(s, axis=1, keepdims=True)
        p = jnp.exp(s - m)
        l = jnp.sum(p, axis=1, keepdims=True)
        o = _dot(p.astype(BF16), vb[:(t + 1) * blk]) / l
        o_ref[rows, :] = o.astype(o_ref.dtype)


def _moba_prompt(proj, k2d, v2d, batch, seq, h_att, q_col0):
    assert seq % MOBA_BLOCK == 0
    n_blocks = seq // MOBA_BLOCK
    assert n_blocks <= HEAD_DIM
    hd = HEAD_DIM
    head = pl.BlockSpec((seq, hd), lambda b, h: (b, h))
    return pl.pallas_call(
        functools.partial(_moba_prompt_kernel, n_blocks=n_blocks),
        grid=(batch, h_att),
        in_specs=[pl.BlockSpec((seq, hd), lambda b, h: (b, q_col0 + h)), head, head],
        out_specs=head,
        out_shape=jax.ShapeDtypeStruct((batch * seq, h_att * hd), BF16),
        compiler_params=_params(32 * MIB, 2),
        name="moba_prompt",
    )(proj, k2d, v2d)


def _row_to_col(x_row, eye):
    n = x_row.shape[1]
    return jnp.sum(jnp.where(eye, jnp.broadcast_to(x_row, (n, n)), 0.0), axis=1, keepdims=True)


def _ret_sample_kernel(q_ref, k_ref, v_ref, g_ref, s_ref, cos_ref, sin_ref, dec_ref, gn_ref,
                       o_ref, so_ref, *, h_ret):
    half = HEAD_DIM // 2
    scale = HEAD_DIM ** -0.5
    cos = cos_ref[...]
    sin = sin_ref[...]
    q = q_ref[0]
    q = q * cos + pltpu.roll(q, half, 1) * sin
    k = k_ref[0]
    k = (k * cos + pltpu.roll(k, half, 1) * sin) * scale
    v = v_ref[0]
    r0 = lax.broadcasted_iota(jnp.int32, (HEAD_DIM, HEAD_DIM), 0)
    r1 = lax.broadcasted_iota(jnp.int32, (HEAD_DIM, HEAD_DIM), 1)
    eye = r0 == r1
    for h in range(h_ret):
        hh = slice(h, h + 1)
        kcol = _row_to_col(k[hh, :], eye)
        qcol = _row_to_col(q[hh, :], eye)
        s_new = s_ref[0, 0, h] * dec_ref[hh, :] + kcol * v[hh, :]
        so_ref[0, 0, h] = s_new
        o = jnp.sum(qcol * s_new, axis=0, keepdims=True)
        o_ref[0, hh, :] = _group_norm_gate(o, gn_ref[hh, :], g_ref[0, hh, :]).astype(o_ref.dtype)


def _ret_sample(proj3, state, layer, gn_w, pos, dec_batch, h_ret):
    hd = HEAD_DIM
    cos2, sin2 = _rotary_tables(pos)
    dec = jnp.exp(1.0 * _log_decay(h_ret))[:, None] * jnp.ones((1, hd), F32)
    gn = gn_w.astype(F32).reshape(h_ret, hd)
    grp = lambda g: pl.BlockSpec((1, h_ret, hd), lambda b: (b, g, 0))
    full = lambda shape: pl.BlockSpec(shape, lambda b: (0,) * len(shape))
    return pl.pallas_call(
        functools.partial(_ret_sample_kernel, h_ret=h_ret),
        grid=(dec_batch,),
        in_specs=[grp(0), grp(1), grp(2), grp(3),
                  pl.BlockSpec((1, 1, h_ret, hd, hd), lambda b: (layer, b, 0, 0, 0)),
                  full((1, hd)), full((1, hd)), full((h_ret, hd)), full((h_ret, hd))],
        out_specs=[pl.BlockSpec((1, h_ret, hd), lambda b: (b, 0, 0)),
                   pl.BlockSpec((1, 1, h_ret, hd, hd), lambda b: (0, b, 0, 0, 0))],
        out_shape=[jax.ShapeDtypeStruct((dec_batch, h_ret, hd), F32),
                   jax.ShapeDtypeStruct((1, dec_batch, h_ret, hd, hd), F32)],
        compiler_params=_params(16 * MIB, 1),
        name="retention_sample",
    )(proj3, proj3, proj3, proj3, state, cos2, sin2, dec, gn)


def _kmean_kernel(pt_ref, *refs, pages_per_block):
    del pt_ref
    o_ref = refs[-1]
    pages = refs[:-1]
    n_out = len(pages) // pages_per_block
    n_heads = pages[0].shape[3]
    inv_n = 1.0 / (pages[0].shape[2] * pages_per_block)
    for i in range(n_out):
        acc = jnp.sum(pages[i * pages_per_block][0, 0], axis=0)
        for p in range(1, pages_per_block):
            acc = acc + jnp.sum(pages[i * pages_per_block + p][0, 0], axis=0)
        acc = acc * inv_n
        for h in range(n_heads):
            o_ref[0, h, i:i + 1, :] = acc[h:h + 1, :]


def _kmean_sample(cache, layer, page_table, pages_per_block):
    db, n_pages = page_table.shape
    page, n_heads, hd = cache.shape[2:]
    pps = KMEAN_PAGES_PER_STEP
    assert n_pages % pps == 0 and pps % pages_per_block == 0
    blocks_per_step = pps // pages_per_block
    assert blocks_per_step % 8 == 0

    def page_spec(i):
        return pl.BlockSpec((1, 1, page, n_heads, hd),
                            lambda b, c, pt: (layer, pt[b, c * pps + i], 0, 0, 0))

    grid_spec = pltpu.PrefetchScalarGridSpec(
        num_scalar_prefetch=1,
        grid=(db, n_pages // pps),
        in_specs=[page_spec(i) for i in range(pps)],
        out_specs=pl.BlockSpec((1, n_heads, blocks_per_step, hd), lambda b, c, pt: (b, 0, c, 0)),
    )
    return pl.pallas_call(
        functools.partial(_kmean_kernel, pages_per_block=pages_per_block),
        grid_spec=grid_spec,
        out_shape=jax.ShapeDtypeStruct((db, n_heads, n_pages // pages_per_block, hd), F32),
        compiler_params=_params(2 * pps * page * n_heads * hd * 4 + 8 * MIB, 2),
        name="moba_sample_kmean",
    )(page_table, *([cache] * pps))


def _select_kernel(km_ref, q_ref, o_ref, *, n_sel):
    n_heads, nb = km_ref.shape[1], km_ref.shape[2]
    blk_id = lax.broadcasted_iota(jnp.int32, (nb, 1), 0).astype(F32)
    row = lax.broadcasted_iota(jnp.int32, o_ref.shape[1:], 0)
    lane = lax.broadcasted_iota(jnp.int32, o_ref.shape[1:], 1)
    out = jnp.zeros(o_ref.shape[1:], jnp.int32)
    for h in range(n_heads):
        gate = jnp.sum(km_ref[0, h] * q_ref[0, h:h + 1, :], axis=1, keepdims=True)
        for r in range(n_sel):
            best = jnp.max(gate, axis=0, keepdims=True)
            idx = jnp.min(jnp.where(gate == best, blk_id, float(nb)), axis=0, keepdims=True)
            out = jnp.where((row == r) & (lane == h), idx.astype(jnp.int32), out)
            gate = jnp.where(blk_id == idx, NEG_INF, gate)
    o_ref[0] = out


def _select_sample(kmean, proj3, q_group, n_sel):
    db, n_heads, nb, hd = kmean.shape
    assert n_sel <= 8 and n_heads <= hd
    return pl.pallas_call(
        functools.partial(_select_kernel, n_sel=n_sel),
        grid=(db,),
        in_specs=[pl.BlockSpec((1, n_heads, nb, hd), lambda b: (b, 0, 0, 0)),
                  pl.BlockSpec((1, n_heads, hd), lambda b: (b, q_group, 0))],
        out_specs=pl.BlockSpec((1, 8, hd), lambda b: (b, 0, 0)),
        out_shape=jax.ShapeDtypeStruct((db, 8, hd), jnp.int32),
        compiler_params=_params(16 * MIB, 1),
        name="moba_sample_select",
    )(kmean, proj3)


def _attn_sample_kernel(pt_ref, sel_ref, q_ref, kn_ref, vn_ref, *refs, n_pages_sel, head_tile):
    del pt_ref, sel_ref
    o_ref = refs[-1]
    k_pages = refs[:n_pages_sel]
    v_pages = refs[n_pages_sel:2 * n_pages_sel]
    hd = HEAD_DIM
    scale = hd ** -0.5
    page = k_pages[0].shape[2]
    rows = page * head_tile
    k_all = jnp.concatenate([kp[0, 0].reshape(rows, hd).astype(BF16) for kp in k_pages], axis=0)
    v_all = jnp.concatenate([vp[0, 0].reshape(rows, hd).astype(BF16) for vp in v_pages], axis=0)
    q = q_ref[0]
    q8 = jnp.broadcast_to(q * scale, (8, hd)).astype(BF16)
    s = _dot_nt(q8, k_all)
    head_in_tile = pl.program_id(1) % head_tile
    col = lax.broadcasted_iota(jnp.int32, s.shape, 1)
    s = jnp.where(col % head_tile == head_in_tile, s, NEG_INF)
    s_own = jnp.sum(kn_ref[0] * q, axis=1, keepdims=True) * scale
    m = jnp.maximum(jnp.max(s, axis=1, keepdims=True), s_own)
    e = jnp.exp(s - m)
    e_own = jnp.exp(s_own - m)
    l = jnp.sum(e, axis=1, keepdims=True) + e_own
    o = _dot(e.astype(BF16), v_all) + e_own * vn_ref[0]
    o_ref[0] = (o / l)[0:1, :]


def _attn_sample(cache_k, cache_v, layer, page_table, sel2, q4, q_stride, q_off, kn4, vn4,
                 dec_batch, n_sel, pages_per_block):
    page, h_att, hd = cache_k.shape[2:]
    head_tile = 8
    assert h_att % head_tile == 0
    n_pages_sel = n_sel * pages_per_block

    def vec_spec(stride, off):
        return pl.BlockSpec((1, 1, hd), lambda b, h, pt, sel: (b * stride + off + h, 0, 0))

    def page_spec(r, i):
        return pl.BlockSpec(
            (1, 1, page, head_tile, hd),
            lambda b, h, pt, sel: (layer, pt[b, sel[b * 8 + r, h] * pages_per_block + i], 0,
                                   h // head_tile, 0))

    page_specs = [page_spec(r, i) for r in range(n_sel) for i in range(pages_per_block)]
    grid_spec = pltpu.PrefetchScalarGridSpec(
        num_scalar_prefetch=2,
        grid=(dec_batch, h_att),
        in_specs=[vec_spec(q_stride, q_off), vec_spec(h_att, 0), vec_spec(h_att, 0)]
        + page_specs + page_specs,
        out_specs=pl.BlockSpec((1, 1, hd), lambda b, h, pt, sel: (b * h_att + h, 0, 0)),
    )
    return pl.pallas_call(
        functools.partial(_attn_sample_kernel, n_pages_sel=n_pages_sel, head_tile=head_tile),
        grid_spec=grid_spec,
        out_shape=jax.ShapeDtypeStruct((dec_batch * h_att, 1, hd), F32),
        compiler_params=_params(
            4 * n_pages_sel * page * head_tile * hd * 4 + 16 * MIB, 2),
        name="moba_sample_attn",
    )(page_table, sel2, q4, kn4, vn4,
      *([cache_k] * n_pages_sel), *([cache_v] * n_pages_sel))


def _pad_rows(x, rows):
    return jnp.pad(x, ((0, rows - x.shape[0]), (0, 0)))


def kernel(x_prompt, x_sample, cache_k, cache_v, state_ret, page_table, w_in, w_out, norm_mix,
           ret_gn, norm_ffn, w_gate, w_up, w_down, norm_final):
    batch, seq, d_model = x_prompt.shape
    dec_batch, dec_seq, _ = x_sample.shape
    depth, n_pool, page_size, h_att, hd = cache_k.shape
    h_ret = state_ret.shape[2]
    w_ret, w_att = h_ret * hd, h_att * hd
    n_pages = page_table.shape[1]
    past_len = n_pages * page_size
    assert hd == HEAD_DIM and h_ret == h_att and dec_seq == 1
    assert MOBA_BLOCK % page_size == 0 and past_len % MOBA_BLOCK == 0
    pages_per_block = MOBA_BLOCK // page_size
    n_past_blocks = past_len // MOBA_BLOCK
    assert n_past_blocks >= MOBA_TOPK and dec_batch <= SAMPLE_ROWS
    n_groups = 7
    att_group = 4
    in_cols = w_in.shape[2]
    assert in_cols == n_groups * w_ret

    m_p = batch * seq
    yp = x_prompt.reshape(m_p, d_model)
    ys = _pad_rows(x_sample.reshape(dec_batch, d_model), SAMPLE_ROWS)
    pos_s = past_len + jnp.arange(dec_seq)

    w_down_b = w_down.astype(BF16)
    n_mix = 5

    rp_states, ks_rows, vs_rows, rs_states = [], [], [], []
    rows5 = (depth, batch, seq, h_att, hd)
    k_prompt = v_prompt = None
    for l in range(depth):
        hp = _rmsnorm(yp, norm_mix[l], BF16, 256)
        hs = _rmsnorm(ys, norm_mix[l], BF16, SAMPLE_ROWS)
        proj_p, proj_s = _proj(hp, hs, w_in, l, n_mix * w_ret, 1024, 512)
        k_prompt, k2d, k_s = _kv_proj(hp, hs, w_in, l, 5 * w_ret, rows5, k_prompt, 512)
        v_prompt, v2d, v_s = _kv_proj(hp, hs, w_in, l, 6 * w_ret, rows5, v_prompt, 512)

        ret_o, ret_s = _ret_prompt(proj_p, ret_gn[l], batch, seq, h_ret)
        att_o = _moba_prompt(proj_p, k2d, v2d, batch, seq, h_att, att_group * h_ret)
        rp_states.append(ret_s)

        proj3 = proj_s.reshape(SAMPLE_ROWS, n_mix * h_ret, hd)
        q4 = proj_s.reshape(SAMPLE_ROWS * n_mix * h_ret, 1, hd)
        kn4 = k_s.reshape(SAMPLE_ROWS * h_att, 1, hd)
        vn4 = v_s.reshape(SAMPLE_ROWS * h_att, 1, hd)
        ret_os, ret_ss = _ret_sample(proj3, state_ret, l, ret_gn[l], pos_s, dec_batch, h_ret)
        kmean = _kmean_sample(cache_k, l, page_table, pages_per_block)
        sel = _select_sample(kmean, proj3, att_group, MOBA_TOPK)
        att_os = _attn_sample(cache_k, cache_v, l, page_table, sel.reshape(dec_batch * 8, hd),
                              q4, n_mix * h_att, att_group * h_att, kn4, vn4,
                              dec_batch, MOBA_TOPK, pages_per_block)
        ks_rows.append(k_s[:dec_batch].reshape(dec_batch, 1, h_att, hd))
        vs_rows.append(v_s[:dec_batch].reshape(dec_batch, 1, h_att, hd))
        rs_states.append(ret_ss[0])
        mix_r = _pad_rows(ret_os.reshape(dec_batch, w_ret), SAMPLE_ROWS).astype(BF16)
        mix_a = _pad_rows(att_os.reshape(dec_batch, w_att), SAMPLE_ROWS).astype(BF16)

        yp, ys = _outproj(ret_o, att_o, mix_r, mix_a, w_out, l, yp, ys, 1024, 512)
        hf = _rmsnorm(yp, norm_ffn[l], BF16, 256)
        hfs = _rmsnorm(ys, norm_ffn[l], BF16, SAMPLE_ROWS)
        act, acts = _gateup(hf, hfs, w_gate, w_up, l, 1024, 256)
        yp, ys = _down(act, acts, w_down_b, l, yp, ys, 512, 512)

    y_prompt = _rmsnorm(yp, norm_final, F32, 256).reshape(batch, seq, d_model)
    y_sample = _rmsnorm(ys, norm_final, F32, SAMPLE_ROWS)[:dec_batch].reshape(dec_batch, 1, d_model)
    return (y_prompt, y_sample, k_prompt, v_prompt, jnp.stack(rp_states),
            jnp.stack(ks_rows), jnp.stack(vs_rows), jnp.stack(rs_states))
```

```python
import functools
import math

import jax
import jax.numpy as jnp
from jax import lax
from jax.experimental import pallas as pl
from jax.experimental.pallas import tpu as pltpu

HEAD_DIM = 128
RET_CHUNK = 128
MOBA_BLOCK = 256
MOBA_TOPK = 3
ROPE_BASE = 10000.0
NORM_EPS = 1e-6
GN_EPS = 1e-5

BF16 = jnp.bfloat16
F32 = jnp.float32
NEG_INF = float("-inf")
MIB = 1024 * 1024
V7X_VMEM_BUDGET = 56 * MIB
SAMPLE_ROWS = 16
KMEAN_PAGES_PER_STEP = 16


def _params(vmem_bytes, n_axes):
    return pltpu.CompilerParams(
        dimension_semantics=("arbitrary",) * n_axes,
        vmem_limit_bytes=int(min(max(vmem_bytes, 16 * MIB), V7X_VMEM_BUDGET)))


def _dot(a, b):
    return jnp.dot(a, b, preferred_element_type=F32)


def _dot_nt(a, b, precision=None):
    return lax.dot_general(a, b, (((1,), (1,)), ((), ())), precision=precision,
                           preferred_element_type=F32)


def _dot_tn(a, b):
    return lax.dot_general(a, b, (((0,), (0,)), ((), ())), preferred_element_type=F32)


def _silu(x):
    return x * (1.0 / (1.0 + jnp.exp(-x)))


def _rmsnorm_kernel(x_ref, g_ref, o_ref):
    x = x_ref[...]
    y = x * lax.rsqrt(jnp.mean(x * x, axis=-1, keepdims=True) + NORM_EPS)
    o_ref[...] = (y * g_ref[...]).astype(o_ref.dtype)


def _rmsnorm(x, g, out_dtype, tm):
    m, d = x.shape
    return pl.pallas_call(
        _rmsnorm_kernel,
        grid=(m // tm,),
        in_specs=[pl.BlockSpec((tm, d), lambda i: (i, 0)),
                  pl.BlockSpec((1, d), lambda i: (0, 0))],
        out_specs=pl.BlockSpec((tm, d), lambda i: (i, 0)),
        out_shape=jax.ShapeDtypeStruct((m, d), out_dtype),
        compiler_params=_params(6 * tm * d * 4, 1),
        name="rmsnorm",
    )(x, g.reshape(1, d))


def _first_m_step():
    return pl.program_id(1) == 0


def _m_tile(n, m, m_tiles):
    return jnp.where(n % 2 == 0, m, m_tiles - 1 - m)


def _a_spec(tm, width, m_tiles):
    return pl.BlockSpec((tm, width), lambda n, m: (_m_tile(n, m, m_tiles), 0))


def _o_spec(tm, tn, m_tiles):
    return pl.BlockSpec((tm, tn), lambda n, m: (_m_tile(n, m, m_tiles), n))


def _weight_spec(k, tn, layer, col_block0=0, single_buffer=False):
    mode = pl.Buffered(1) if single_buffer else None
    return pl.BlockSpec((None, k, tn), lambda n, m: (layer, 0, col_block0 + n), pipeline_mode=mode)


def _proj_kernel(a_ref, as_ref, w_ref, o_ref, os_ref, wb_ref):
    @pl.when(_first_m_step())
    def _():
        wb_ref[...] = w_ref[...].astype(BF16)
        os_ref[...] = _dot(as_ref[...], wb_ref[...])

    o_ref[...] = _dot(a_ref[...], wb_ref[...])


def _proj(a, a_s, w, layer, n_cols, tm, tn):
    m, k = a.shape
    rows_s = a_s.shape[0]
    assert m % tm == 0 and n_cols % tn == 0
    return pl.pallas_call(
        _proj_kernel,
        grid=(n_cols // tn, m // tm),
        in_specs=[_a_spec(tm, k, m // tm),
                  pl.BlockSpec((rows_s, k), lambda n, m: (0, 0)),
                  _weight_spec(k, tn, layer)],
        out_specs=[_o_spec(tm, tn, m // tm),
                   pl.BlockSpec((rows_s, tn), lambda n, m: (0, n))],
        out_shape=[jax.ShapeDtypeStruct((m, n_cols), F32),
                   jax.ShapeDtypeStruct((rows_s, n_cols), F32)],
        scratch_shapes=[pltpu.VMEM((k, tn), BF16)],
        compiler_params=_params(2 * (tm * k * 2 + k * tn * 4 + tm * tn * 4) + k * tn * 2 + 8 * MIB, 2),
        name="in_proj",
    )(a, a_s, w)


def _kv_proj_kernel(*refs):
    a_ref, as_ref, w_ref = refs[:3]
    o5_ref, o2_ref, os_ref, wb_ref = refs[-4:]

    @pl.when(_first_m_step())
    def _():
        wb_ref[...] = w_ref[...].astype(BF16)
        os_ref[...] = _dot(as_ref[...], wb_ref[...])

    acc = _dot(a_ref[...], wb_ref[...])
    o2_ref[...] = acc.astype(o2_ref.dtype)
    for h in range(o5_ref.shape[3]):
        o5_ref[0, 0, :, h, :] = acc[:, h * HEAD_DIM:(h + 1) * HEAD_DIM]


def _kv_proj(a, a_s, w, layer, col0, rows5, prev, tm):
    m, k = a.shape
    rows_s = a_s.shape[0]
    depth, batch, seq, n_heads, hd = rows5
    heads_per_step = 8
    tn = heads_per_step * hd
    assert seq % tm == 0 and n_heads % heads_per_step == 0 and col0 % tn == 0
    tiles_per_seq = seq // tm
    m_tiles = m // tm
    in_specs = [_a_spec(tm, k, m // tm),
                pl.BlockSpec((rows_s, k), lambda n, m: (0, 0)),
                _weight_spec(k, tn, layer, col0 // tn, single_buffer=True)]
    args = [a, a_s, w]
    aliases = {}
    if prev is not None:
        in_specs.append(pl.BlockSpec(memory_space=pl.ANY))
        args.append(prev)
        aliases = {3: 0}
    return pl.pallas_call(
        _kv_proj_kernel,
        grid=(n_heads // heads_per_step, m // tm),
        in_specs=in_specs,
        out_specs=[pl.BlockSpec((1, 1, tm, heads_per_step, hd),
                                lambda n, m: (layer, _m_tile(n, m, m_tiles) // tiles_per_seq,
                                              _m_tile(n, m, m_tiles) % tiles_per_seq, n, 0)),
                   _o_spec(tm, tn, m // tm),
                   pl.BlockSpec((rows_s, tn), lambda n, m: (0, n))],
        out_shape=[jax.ShapeDtypeStruct(rows5, F32),
                   jax.ShapeDtypeStruct((m, n_heads * hd), BF16),
                   jax.ShapeDtypeStruct((rows_s, n_heads * hd), F32)],
        scratch_shapes=[pltpu.VMEM((k, tn), BF16)],
        input_output_aliases=aliases,
        compiler_params=_params(
            2 * (tm * k * 2 + tm * tn * 6) + k * tn * 6 + 8 * MIB, 2),
        name="kv_proj",
    )(*args)


def _outproj_kernel(a1_ref, a2_ref, a1s_ref, a2s_ref, w_ref, r_ref, rs_ref, o_ref, os_ref, wb_ref):
    k1 = a1_ref.shape[1]

    def mix(x1_ref, x2_ref, res_ref):
        return res_ref[...] + (_dot(x1_ref[...], wb_ref[:k1, :]) + _dot(x2_ref[...], wb_ref[k1:, :]))

    @pl.when(_first_m_step())
    def _():
        wb_ref[...] = w_ref[...].astype(BF16)
        os_ref[...] = mix(a1s_ref, a2s_ref, rs_ref)

    o_ref[...] = mix(a1_ref, a2_ref, r_ref)


def _outproj(a1, a2, a1_s, a2_s, w, layer, res, res_s, tm, tn):
    m, k1 = a1.shape
    k2 = a2.shape[1]
    rows_s = a1_s.shape[0]
    k, n = k1 + k2, w.shape[2]
    assert m % tm == 0 and n % tn == 0
    return pl.pallas_call(
        _outproj_kernel,
        grid=(n // tn, m // tm),
        in_specs=[_a_spec(tm, k1, m // tm),
                  _a_spec(tm, k2, m // tm),
                  pl.BlockSpec((rows_s, k1), lambda n, m: (0, 0)),
                  pl.BlockSpec((rows_s, k2), lambda n, m: (0, 0)),
                  _weight_spec(k, tn, layer),
                  _o_spec(tm, tn, m // tm),
                  pl.BlockSpec((rows_s, tn), lambda n, m: (0, n))],
        out_specs=[_o_spec(tm, tn, m // tm),
                   pl.BlockSpec((rows_s, tn), lambda n, m: (0, n))],
        out_shape=[jax.ShapeDtypeStruct((m, n), F32),
                   jax.ShapeDtypeStruct((rows_s, n), F32)],
        scratch_shapes=[pltpu.VMEM((k, tn), BF16)],
        compiler_params=_params(
            2 * (tm * k * 2 + k * tn * 4 + 2 * tm * tn * 4) + k * tn * 2 + 8 * MIB, 2),
        name="out_proj",
    )(a1, a2, a1_s, a2_s, w, res, res_s)


def _gateup_kernel(h_ref, hs_ref, wg_ref, wu_ref, o_ref, os_ref, wgb_ref, wub_ref):
    def swiglu_in(x_ref):
        x = x_ref[...]
        return _silu(_dot(x, wgb_ref[...])) * _dot(x, wub_ref[...])

    @pl.when(_first_m_step())
    def _():
        wgb_ref[...] = wg_ref[...].astype(BF16)
        wub_ref[...] = wu_ref[...].astype(BF16)
        os_ref[...] = swiglu_in(hs_ref).astype(os_ref.dtype)

    o_ref[...] = swiglu_in(h_ref).astype(o_ref.dtype)


def _gateup(h, h_s, wg, wu, layer, tm, tn):
    m, k = h.shape
    rows_s = h_s.shape[0]
    n = wg.shape[2]
    assert m % tm == 0 and n % tn == 0
    return pl.pallas_call(
        _gateup_kernel,
        grid=(n // tn, m // tm),
        in_specs=[_a_spec(tm, k, m // tm),
                  pl.BlockSpec((rows_s, k), lambda n, m: (0, 0)),
                  _weight_spec(k, tn, layer), _weight_spec(k, tn, layer)],
        out_specs=[_o_spec(tm, tn, m // tm),
                   pl.BlockSpec((rows_s, tn), lambda n, m: (0, n))],
        out_shape=[jax.ShapeDtypeStruct((m, n), BF16),
                   jax.ShapeDtypeStruct((rows_s, n), BF16)],
        scratch_shapes=[pltpu.VMEM((k, tn), BF16), pltpu.VMEM((k, tn), BF16)],
        compiler_params=_params(
            2 * (tm * k * 2 + 2 * k * tn * 4 + tm * tn * 2) + 2 * k * tn * 2 + 8 * MIB, 2),
        name="ffn_gate_up",
    )(h, h_s, wg, wu)


def _down_kernel(a_ref, as_ref, w_ref, r_ref, rs_ref, o_ref, os_ref):
    @pl.when(_first_m_step())
    def _():
        os_ref[...] = rs_ref[...] + _dot(as_ref[...], w_ref[...])

    o_ref[...] = r_ref[...] + _dot(a_ref[...], w_ref[...])


def _down(a, a_s, w_bf16, layer, res, res_s, tm, tn):
    m, k = a.shape
    rows_s = a_s.shape[0]
    n = w_bf16.shape[2]
    assert m % tm == 0 and n % tn == 0
    return pl.pallas_call(
        _down_kernel,
        grid=(n // tn, m // tm),
        in_specs=[_a_spec(tm, k, m // tm),
                  pl.BlockSpec((rows_s, k), lambda n, m: (0, 0)),
                  _weight_spec(k, tn, layer),
                  _o_spec(tm, tn, m // tm),
                  pl.BlockSpec((rows_s, tn), lambda n, m: (0, n))],
        out_specs=[_o_spec(tm, tn, m // tm),
                   pl.BlockSpec((rows_s, tn), lambda n, m: (0, n))],
        out_shape=[jax.ShapeDtypeStruct((m, n), F32),
                   jax.ShapeDtypeStruct((rows_s, n), F32)],
        compiler_params=_params(2 * (tm * k * 2 + k * tn * 2 + 2 * tm * tn * 4) + 8 * MIB, 2),
        name="ffn_down",
    )(a, a_s, w_bf16, res, res_s)


def _rotary_tables(pos):
    half = HEAD_DIM // 2
    inv = ROPE_BASE ** (-jnp.arange(half, dtype=F32) / half)
    ang = pos.astype(F32)[:, None] * inv[None, :]
    cos, sin = jnp.cos(ang), jnp.sin(ang)
    return (jnp.concatenate([cos, cos], axis=-1), jnp.concatenate([-sin, sin], axis=-1))


def _log_decay(h_ret):
    return jnp.log1p(-jnp.exp2(-5.0 - jnp.arange(h_ret, dtype=F32)))


def _group_norm_gate(o, gn, g):
    mu = jnp.mean(o, axis=-1, keepdims=True)
    d = o - mu
    var = jnp.mean(d * d, axis=-1, keepdims=True)
    return _silu(g) * (d * lax.rsqrt(var + GN_EPS) * gn)


def _ret_prompt_kernel(q_ref, k_ref, v_ref, g_ref, cos_ref, sin_ref, intra_ref, qdec_ref,
                       kdec_ref, cdec_ref, gn_ref, o_ref, s_ref,
                       qb_ref, kb_ref, qd_ref, kd_ref, vb_ref, of_ref, *, chunk, n_chunks, unroll):
    half = HEAD_DIM // 2
    scale = HEAD_DIM ** -0.5
    seq = q_ref.shape[0]
    intra = intra_ref[0]
    cdec = cdec_ref[0, 0:1, :]

    cos = cos_ref[...]
    sin = sin_ref[...]
    q = q_ref[...]
    q = q * cos + pltpu.roll(q, half, 1) * sin
    k = k_ref[...]
    k = (k * cos + pltpu.roll(k, half, 1) * sin) * scale
    qb_ref[...] = q.astype(BF16)
    kb_ref[...] = k.astype(BF16)
    qd = q.reshape(n_chunks, chunk, HEAD_DIM) * qdec_ref[...]
    kd = k.reshape(n_chunks, chunk, HEAD_DIM) * kdec_ref[...]
    qd_ref[...] = qd.reshape(seq, HEAD_DIM).astype(BF16)
    kd_ref[...] = kd.reshape(seq, HEAD_DIM).astype(BF16)
    vb_ref[...] = v_ref[...].astype(BF16)

    def body(c, s):
        r = pl.ds(pl.multiple_of(c * chunk, chunk), chunk)
        v = vb_ref[r, :]
        sc = _dot_nt(qb_ref[r, :], kb_ref[r, :]) * intra
        of_ref[r, :] = _dot(sc.astype(BF16), v) + _dot(qd_ref[r, :], s.astype(BF16))
        return s * cdec + _dot_tn(kd_ref[r, :], v)

    s_fin = lax.fori_loop(0, n_chunks, body, jnp.zeros((HEAD_DIM, HEAD_DIM), F32),
                          unroll=unroll)
    s_ref[0, 0] = s_fin
    o_ref[...] = _group_norm_gate(of_ref[...], gn_ref[0], g_ref[...]).astype(o_ref.dtype)


def _ret_prompt(proj, gn_w, batch, seq, h_ret):
    chunk = math.gcd(seq, RET_CHUNK)
    n_chunks = seq // chunk
    cos2, sin2 = _rotary_tables(jnp.arange(seq))
    lg = _log_decay(h_ret)
    i = jnp.arange(chunk, dtype=F32)
    rel = i[:, None] - i[None, :]
    intra = jnp.exp(jnp.where(rel[None] >= 0, rel[None] * lg[:, None, None], -jnp.inf))
    ones = jnp.ones((1, 1, HEAD_DIM), F32)
    qdec = jnp.exp((i + 1.0)[None, :, None] * lg[:, None, None]) * ones
    kdec = jnp.exp((chunk - 1.0 - i)[None, :, None] * lg[:, None, None]) * ones
    cdec = jnp.exp(chunk * lg)[:, None, None] * jnp.ones((1, 8, HEAD_DIM), F32)
    gn = gn_w.astype(F32).reshape(h_ret, 1, HEAD_DIM)
    hd = HEAD_DIM

    def col(group):
        return pl.BlockSpec((seq, hd), lambda b, h: (b, group * h_ret + h))

    const = lambda shape: pl.BlockSpec(shape, lambda b, h: (h, 0, 0))
    unroll = math.gcd(n_chunks, 16)
    return pl.pallas_call(
        functools.partial(_ret_prompt_kernel, chunk=chunk, n_chunks=n_chunks, unroll=unroll),
        grid=(batch, h_ret),
        in_specs=[col(0), col(1), col(2), col(3),
                  pl.BlockSpec((seq, hd), lambda b, h: (0, 0)),
                  pl.BlockSpec((seq, hd), lambda b, h: (0, 0)),
                  const((1, chunk, chunk)), const((1, chunk, hd)), const((1, chunk, hd)),
                  const((1, 8, hd)), const((1, 1, hd))],
        out_specs=[pl.BlockSpec((seq, hd), lambda b, h: (b, h)),
                   pl.BlockSpec((1, 1, hd, hd), lambda b, h: (b, h, 0, 0))],
        out_shape=[jax.ShapeDtypeStruct((batch * seq, h_ret * hd), BF16),
                   jax.ShapeDtypeStruct((batch, h_ret, hd, hd), F32)],
        scratch_shapes=[pltpu.VMEM((seq, hd), BF16)] * 5 + [pltpu.VMEM((seq, hd), F32)],
        compiler_params=_params(2 * 7 * seq * hd * 4 + 16 * MIB, 2),
        name="retention_prompt",
    )(proj, proj, proj, proj, cos2, sin2, intra, qdec, kdec, cdec, gn)


def _block_keep(gate_t, n_past, eye_bf16):
    blk_id = lax.broadcasted_iota(jnp.int32, gate_t.shape, 0)
    g = jnp.where(blk_id < n_past, gate_t, NEG_INF)
    rank = jnp.zeros(gate_t.shape, F32)
    for i in range(n_past):
        gi = g[i:i + 1, :]
        rank = rank + jnp.where(gi > g, 1.0,
                                jnp.where(gi == g, jnp.where(blk_id > i, 1.0, 0.0), 0.0))
    keep_t = jnp.where(blk_id < n_past, jnp.where(rank < float(MOBA_TOPK), 1.0, 0.0), 0.0)
    pad = jnp.zeros((HEAD_DIM - gate_t.shape[0], gate_t.shape[1]), F32)
    keep_t = jnp.concatenate([keep_t, pad], axis=0).astype(BF16)
    return _dot_tn(keep_t, eye_bf16)


def _moba_prompt_kernel(q_ref, k_ref, v_ref, o_ref, *, n_blocks):
    blk = MOBA_BLOCK
    scale = HEAD_DIM ** -0.5
    kb = k_ref[...]
    vb = v_ref[...]
    kmean = jnp.mean(kb.astype(F32).reshape(n_blocks, blk, HEAD_DIM), axis=1)
    n_rank_rows = -(-n_blocks // 8) * 8
    kmean_pad = jnp.concatenate(
        [kmean, jnp.zeros((HEAD_DIM - n_blocks, HEAD_DIM), F32)], axis=0)
    row = lax.broadcasted_iota(jnp.int32, (blk, blk), 0)
    colv = lax.broadcasted_iota(jnp.int32, (blk, blk), 1)
    causal = colv <= row
    r128 = lax.broadcasted_iota(jnp.int32, (HEAD_DIM, HEAD_DIM), 0)
    c128 = lax.broadcasted_iota(jnp.int32, (HEAD_DIM, HEAD_DIM), 1)
    eye_bf16 = jnp.where(r128 == c128, 1.0, 0.0).astype(BF16)

    for t in range(n_blocks):
        rows = slice(t * blk, (t + 1) * blk)
        qf = q_ref[rows, :]
        s = _dot_nt((qf * scale).astype(BF16), kb[:(t + 1) * blk])
        pieces = []
        if t > MOBA_TOPK:
            gate_t = _dot_nt(kmean_pad, qf, precision=lax.Precision.HIGHEST)
            keep = _block_keep(gate_t[:n_rank_rows], t, eye_bf16)
        for j in range(t):
            sj = s[:, j * blk:(j + 1) * blk]
            if t > MOBA_TOPK:
                sj = jnp.where(keep[:, j:j + 1] > 0.5, sj, NEG_INF)
            pieces.append(sj)
        pieces.append(jnp.where(causal, s[:, t * blk:], NEG_INF))
        s = jnp.concatenate(pieces, axis=1) if len(pieces) > 1 else pieces[0]
        m = jnp.max(s, axis=1, keepdims=True)
        p = jnp.exp(s - m)
        l = jnp.sum(p, axis=1, keepdims=True)
        o = _dot(p.astype(BF16), vb[:(t + 1) * blk]) / l
        o_ref[rows, :] = o.astype(o_ref.dtype)


def _moba_prompt(proj, k2d, v2d, batch, seq, h_att, q_col0):
    assert seq % MOBA_BLOCK == 0
    n_blocks = seq // MOBA_BLOCK
    assert n_blocks <= HEAD_DIM
    hd = HEAD_DIM
    head = pl.BlockSpec((seq, hd), lambda b, h: (b, h))
    return pl.pallas_call(
        functools.partial(_moba_prompt_kernel, n_blocks=n_blocks),
        grid=(batch, h_att),
        in_specs=[pl.BlockSpec((seq, hd), lambda b, h: (b, q_col0 + h)), head, head],
        out_specs=head,
        out_shape=jax.ShapeDtypeStruct((batch * seq, h_att * hd), BF16),
        compiler_params=_params(32 * MIB, 2),
        name="moba_prompt",
    )(proj, k2d, v2d)


def _row_to_col(x_row, eye):
    n = x_row.shape[1]
    return jnp.sum(jnp.where(eye, jnp.broadcast_to(x_row, (n, n)), 0.0), axis=1, keepdims=True)


def _ret_sample_kernel(q_ref, k_ref, v_ref, g_ref, s_ref, cos_ref, sin_ref, dec_ref, gn_ref,
                       o_ref, so_ref, *, h_ret):
    half = HEAD_DIM // 2
    scale = HEAD_DIM ** -0.5
    cos = cos_ref[...]
    sin = sin_ref[...]
    q = q_ref[0]
    q = q * cos + pltpu.roll(q, half, 1) * sin
    k = k_ref[0]
    k = (k * cos + pltpu.roll(k, half, 1) * sin) * scale
    v = v_ref[0]
    r0 = lax.broadcasted_iota(jnp.int32, (HEAD_DIM, HEAD_DIM), 0)
    r1 = lax.broadcasted_iota(jnp.int32, (HEAD_DIM, HEAD_DIM), 1)
    eye = r0 == r1
    for h in range(h_ret):
        hh = slice(h, h + 1)
        kcol = _row_to_col(k[hh, :], eye)
        qcol = _row_to_col(q[hh, :], eye)
        s_new = s_ref[0, 0, h] * dec_ref[hh, :] + kcol * v[hh, :]
        so_ref[0, 0, h] = s_new
        o = jnp.sum(qcol * s_new, axis=0, keepdims=True)
        o_ref[0, hh, :] = _group_norm_gate(o, gn_ref[hh, :], g_ref[0, hh, :]).astype(o_ref.dtype)


def _ret_sample(proj3, state, layer, gn_w, pos, dec_batch, h_ret):
    hd = HEAD_DIM
    cos2, sin2 = _rotary_tables(pos)
    dec = jnp.exp(1.0 * _log_decay(h_ret))[:, None] * jnp.ones((1, hd), F32)
    gn = gn_w.astype(F32).reshape(h_ret, hd)
    grp = lambda g: pl.BlockSpec((1, h_ret, hd), lambda b: (b, g, 0))
    full = lambda shape: pl.BlockSpec(shape, lambda b: (0,) * len(shape))
    return pl.pallas_call(
        functools.partial(_ret_sample_kernel, h_ret=h_ret),
        grid=(dec_batch,),
        in_specs=[grp(0), grp(1), grp(2), grp(3),
                  pl.BlockSpec((1, 1, h_ret, hd, hd), lambda b: (layer, b, 0, 0, 0)),
                  full((1, hd)), full((1, hd)), full((h_ret, hd)), full((h_ret, hd))],
        out_specs=[pl.BlockSpec((1, h_ret, hd), lambda b: (b, 0, 0)),
                   pl.BlockSpec((1, 1, h_ret, hd, hd), lambda b: (0, b, 0, 0, 0))],
        out_shape=[jax.ShapeDtypeStruct((dec_batch, h_ret, hd), F32),
                   jax.ShapeDtypeStruct((1, dec_batch, h_ret, hd, hd), F32)],
        compiler_params=_params(16 * MIB, 1),
        name="retention_sample",
    )(proj3, proj3, proj3, proj3, state, cos2, sin2, dec, gn)


def _kmean_kernel(pt_ref, *refs, pages_per_block):
    del pt_ref
    o_ref = refs[-1]
    pages = refs[:-1]
    n_out = len(pages) // pages_per_block
    n_heads = pages[0].shape[3]
    inv_n = 1.0 / (pages[0].shape[2] * pages_per_block)
    for i in range(n_out):
        acc = jnp.sum(pages[i * pages_per_block][0, 0], axis=0)
        for p in range(1, pages_per_block):
            acc = acc + jnp.sum(pages[i * pages_per_block + p][0, 0], axis=0)
        acc = acc * inv_n
        for h in range(n_heads):
            o_ref[0, h, i:i + 1, :] = acc[h:h + 1, :]


def _kmean_sample(cache, layer, page_table, pages_per_block):
    db, n_pages = page_table.shape
    page, n_heads, hd = cache.shape[2:]
    pps = KMEAN_PAGES_PER_STEP
    assert n_pages % pps == 0 and pps % pages_per_block == 0
    blocks_per_step = pps // pages_per_block
    assert blocks_per_step % 8 == 0

    def page_spec(i):
        return pl.BlockSpec((1, 1, page, n_heads, hd),
                            lambda b, c, pt: (layer, pt[b, c * pps + i], 0, 0, 0))

    grid_spec = pltpu.PrefetchScalarGridSpec(
        num_scalar_prefetch=1,
        grid=(db, n_pages // pps),
        in_specs=[page_spec(i) for i in range(pps)],
        out_specs=pl.BlockSpec((1, n_heads, blocks_per_step, hd), lambda b, c, pt: (b, 0, c, 0)),
    )
    return pl.pallas_call(
        functools.partial(_kmean_kernel, pages_per_block=pages_per_block),
        grid_spec=grid_spec,
        out_shape=jax.ShapeDtypeStruct((db, n_heads, n_pages // pages_per_block, hd), F32),
        compiler_params=_params(2 * pps * page * n_heads * hd * 4 + 8 * MIB, 2),
        name="moba_sample_kmean",
    )(page_table, *([cache] * pps))


def _select_kernel(km_ref, q_ref, o_ref, *, n_sel):
    n_heads, nb = km_ref.shape[1], km_ref.shape[2]
    blk_id = lax.broadcasted_iota(jnp.int32, (nb, 1), 0).astype(F32)
    row = lax.broadcasted_iota(jnp.int32, o_ref.shape[1:], 0)
    lane = lax.broadcasted_iota(jnp.int32, o_ref.shape[1:], 1)
    out = jnp.zeros(o_ref.shape[1:], jnp.int32)
    for h in range(n_heads):
        gate = jnp.sum(km_ref[0, h] * q_ref[0, h:h + 1, :], axis=1, keepdims=True)
        for r in range(n_sel):
            best = jnp.max(gate, axis=0, keepdims=True)
            idx = jnp.min(jnp.where(gate == best, blk_id, float(nb)), axis=0, keepdims=True)
            out = jnp.where((row == r) & (lane == h), idx.astype(jnp.int32), out)
            gate = jnp.where(blk_id == idx, NEG_INF, gate)
    o_ref[0] = out


def _select_sample(kmean, proj3, q_group, n_sel):
    db, n_heads, nb, hd = kmean.shape
    assert n_sel <= 8 and n_heads <= hd
    return pl.pallas_call(
        functools.partial(_select_kernel, n_sel=n_sel),
        grid=(db,),
        in_specs=[pl.BlockSpec((1, n_heads, nb, hd), lambda b: (b, 0, 0, 0)),
                  pl.BlockSpec((1, n_heads, hd), lambda b: (b, q_group, 0))],
        out_specs=pl.BlockSpec((1, 8, hd), lambda b: (b, 0, 0)),
        out_shape=jax.ShapeDtypeStruct((db, 8, hd), jnp.int32),
        compiler_params=_params(16 * MIB, 1),
        name="moba_sample_select",
    )(kmean, proj3)


def _attn_sample_kernel(pt_ref, sel_ref, q_ref, kn_ref, vn_ref, ck_ref, cv_ref, o_ref,
                        kbuf, vbuf, sems, *, layer, n_sel, pages_per_block):
    b = pl.program_id(0)
    n_heads, n_pages_sel, page, hd = kbuf.shape
    scale = hd ** -0.5

    def head_copies(h):
        copies = []
        for r in range(n_sel):
            blk = sel_ref[b * 8 + r, h]
            for i in range(pages_per_block):
                phys = pt_ref[b, blk * pages_per_block + i]
                slot = r * pages_per_block + i
                copies.append(pltpu.make_async_copy(
                    ck_ref.at[layer, phys, :, h, :], kbuf.at[h, slot], sems.at[0, h]))
                copies.append(pltpu.make_async_copy(
                    cv_ref.at[layer, phys, :, h, :], vbuf.at[h, slot], sems.at[1, h]))
        return copies

    for h in range(n_heads):
        for c in head_copies(h):
            c.start()

    for h in range(n_heads):
        for c in head_copies(h):
            c.wait()
        hh = slice(h, h + 1)
        q = q_ref[0, hh, :]
        k_all = kbuf[h].reshape(n_pages_sel * page, hd).astype(BF16)
        v_all = vbuf[h].reshape(n_pages_sel * page, hd).astype(BF16)
        q8 = jnp.broadcast_to(q * scale, (8, hd)).astype(BF16)
        s = _dot_nt(q8, k_all)
        s_own = jnp.sum(kn_ref[0, hh, :] * q, axis=1, keepdims=True) * scale
        m = jnp.maximum(jnp.max(s, axis=1, keepdims=True), s_own)
        e = jnp.exp(s - m)
        e_own = jnp.exp(s_own - m)
        l = jnp.sum(e, axis=1, keepdims=True) + e_own
        o = _dot(e.astype(BF16), v_all) + e_own * vn_ref[0, hh, :]
        o_ref[0, hh, :] = (o / l)[0:1, :]


def _attn_sample(cache_k, cache_v, layer, page_table, sel2, q3, q_group, kn3, vn3,
                 dec_batch, n_sel, pages_per_block):
    page, h_att, hd = cache_k.shape[2:]
    n_pages_sel = n_sel * pages_per_block
    grid_spec = pltpu.PrefetchScalarGridSpec(
        num_scalar_prefetch=2,
        grid=(dec_batch,),
        in_specs=[pl.BlockSpec((1, h_att, hd), lambda b, pt, sel: (b, q_group, 0)),
                  pl.BlockSpec((1, h_att, hd), lambda b, pt, sel: (b, 0, 0)),
                  pl.BlockSpec((1, h_att, hd), lambda b, pt, sel: (b, 0, 0)),
                  pl.BlockSpec(memory_space=pl.ANY),
                  pl.BlockSpec(memory_space=pl.ANY)],
        out_specs=pl.BlockSpec((1, h_att, hd), lambda b, pt, sel: (b, 0, 0)),
        scratch_shapes=[pltpu.VMEM((h_att, n_pages_sel, page, hd), F32),
                        pltpu.VMEM((h_att, n_pages_sel, page, hd), F32),
                        pltpu.SemaphoreType.DMA((2, h_att))],
    )
    return pl.pallas_call(
        functools.partial(_attn_sample_kernel, layer=layer, n_sel=n_sel,
                          pages_per_block=pages_per_block),
        grid_spec=grid_spec,
        out_shape=jax.ShapeDtypeStruct((dec_batch, h_att, hd), F32),
        compiler_params=_params(2 * h_att * n_pages_sel * page * hd * 4 + 16 * MIB, 1),
        name="moba_sample_attn",
    )(page_table, sel2, q3, kn3, vn3, cache_k, cache_v)


def _pad_rows(x, rows):
    return jnp.pad(x, ((0, rows - x.shape[0]), (0, 0)))


def kernel(x_prompt, x_sample, cache_k, cache_v, state_ret, page_table, w_in, w_out, norm_mix,
           ret_gn, norm_ffn, w_gate, w_up, w_down, norm_final):
    batch, seq, d_model = x_prompt.shape
    dec_batch, dec_seq, _ = x_sample.shape
    depth, n_pool, page_size, h_att, hd = cache_k.shape
    h_ret = state_ret.shape[2]
    w_ret, w_att = h_ret * hd, h_att * hd
    n_pages = page_table.shape[1]
    past_len = n_pages * page_size
    assert hd == HEAD_DIM and h_ret == h_att and dec_seq == 1
    assert MOBA_BLOCK % page_size == 0 and past_len % MOBA_BLOCK == 0
    pages_per_block = MOBA_BLOCK // page_size
    n_past_blocks = past_len // MOBA_BLOCK
    assert n_past_blocks >= MOBA_TOPK and dec_batch <= SAMPLE_ROWS
    n_groups = 7
    att_group = 4
    in_cols = w_in.shape[2]
    assert in_cols == n_groups * w_ret

    m_p = batch * seq
    yp = x_prompt.reshape(m_p, d_model)
    ys = _pad_rows(x_sample.reshape(dec_batch, d_model), SAMPLE_ROWS)
    pos_s = past_len + jnp.arange(dec_seq)

    w_down_b = w_down.astype(BF16)
    n_mix = 5

    rp_states, ks_rows, vs_rows, rs_states = [], [], [], []
    rows5 = (depth, batch, seq, h_att, hd)
    k_prompt = v_prompt = None
    for l in range(depth):
        hp = _rmsnorm(yp, norm_mix[l], BF16, 256)
        hs = _rmsnorm(ys, norm_mix[l], BF16, SAMPLE_ROWS)
        proj_p, proj_s = _proj(hp, hs, w_in, l, n_mix * w_ret, 1024, 512)
        k_prompt, k2d, k_s = _kv_proj(hp, hs, w_in, l, 5 * w_ret, rows5, k_prompt, 512)
        v_prompt, v2d, v_s = _kv_proj(hp, hs, w_in, l, 6 * w_ret, rows5, v_prompt, 512)

        ret_o, ret_s = _ret_prompt(proj_p, ret_gn[l], batch, seq, h_ret)
        att_o = _moba_prompt(proj_p, k2d, v2d, batch, seq, h_att, att_group * h_ret)
        rp_states.append(ret_s)

        proj3 = proj_s.reshape(SAMPLE_ROWS, n_mix * h_ret, hd)
        kn3 = k_s.reshape(SAMPLE_ROWS, h_att, hd)
        vn3 = v_s.reshape(SAMPLE_ROWS, h_att, hd)
        ret_os, ret_ss = _ret_sample(proj3, state_ret, l, ret_gn[l], pos_s, dec_batch, h_ret)
        kmean = _kmean_sample(cache_k, l, page_table, pages_per_block)
        sel = _select_sample(kmean, proj3, att_group, MOBA_TOPK)
        att_os = _attn_sample(cache_k, cache_v, l, page_table, sel.reshape(dec_batch * 8, hd),
                              proj3, att_group, kn3, vn3, dec_batch, MOBA_TOPK, pages_per_block)
        ks_rows.append(k_s[:dec_batch].reshape(dec_batch, 1, h_att, hd))
        vs_rows.append(v_s[:dec_batch].reshape(dec_batch, 1, h_att, hd))
        rs_states.append(ret_ss[0])
        mix_r = _pad_rows(ret_os.reshape(dec_batch, w_ret), SAMPLE_ROWS).astype(BF16)
        mix_a = _pad_rows(att_os.reshape(dec_batch, w_att), SAMPLE_ROWS).astype(BF16)

        yp, ys = _outproj(ret_o, att_o, mix_r, mix_a, w_out, l, yp, ys, 1024, 512)
        hf = _rmsnorm(yp, norm_ffn[l], BF16, 256)
        hfs = _rmsnorm(ys, norm_ffn[l], BF16, SAMPLE_ROWS)
        act, acts = _gateup(hf, hfs, w_gate, w_up, l, 1024, 256)
        yp, ys = _down(act, acts, w_down_b, l, yp, ys, 512, 512)

    y_prompt = _rmsnorm(yp, norm_final, F32, 256).reshape(batch, seq, d_model)
    y_sample = _rmsnorm(ys, norm_final, F32, SAMPLE_ROWS)[:dec_batch].reshape(dec_batch, 1, d_model)
    return (y_prompt, y_sample, k_prompt, v_prompt, jnp.stack(rp_states),
            jnp.stack(ks_rows), jnp.stack(vs_rows), jnp.stack(rs_states))
```

```python
import functools
import math

import jax
import jax.numpy as jnp
from jax import lax
from jax.experimental import pallas as pl
from jax.experimental.pallas import tpu as pltpu

HEAD_DIM = 128
RET_CHUNK = 128
MOBA_BLOCK = 256
MOBA_TOPK = 3
ROPE_BASE = 10000.0
NORM_EPS = 1e-6
GN_EPS = 1e-5

BF16 = jnp.bfloat16
F32 = jnp.float32
NEG_INF = float("-inf")
MIB = 1024 * 1024
V7X_VMEM_BUDGET = 56 * MIB
SAMPLE_ROWS = 16
KMEAN_PAGES_PER_STEP = 16


def _params(vmem_bytes, n_axes):
    return pltpu.CompilerParams(
        dimension_semantics=("arbitrary",) * n_axes,
        vmem_limit_bytes=int(min(max(vmem_bytes, 16 * MIB), V7X_VMEM_BUDGET)))


def _dot(a, b):
    return jnp.dot(a, b, preferred_element_type=F32)


def _dot_nt(a, b, precision=None):
    return lax.dot_general(a, b, (((1,), (1,)), ((), ())), precision=precision,
                           preferred_element_type=F32)


def _dot_tn(a, b):
    return lax.dot_general(a, b, (((0,), (0,)), ((), ())), preferred_element_type=F32)


def _silu(x):
    return x * (1.0 / (1.0 + jnp.exp(-x)))


def _rmsnorm_kernel(x_ref, g_ref, o_ref):
    x = x_ref[...]
    y = x * lax.rsqrt(jnp.mean(x * x, axis=-1, keepdims=True) + NORM_EPS)
    o_ref[...] = (y * g_ref[...]).astype(o_ref.dtype)


def _rmsnorm(x, g, out_dtype, tm):
    m, d = x.shape
    return pl.pallas_call(
        _rmsnorm_kernel,
        grid=(m // tm,),
        in_specs=[pl.BlockSpec((tm, d), lambda i: (i, 0)),
                  pl.BlockSpec((1, d), lambda i: (0, 0))],
        out_specs=pl.BlockSpec((tm, d), lambda i: (i, 0)),
        out_shape=jax.ShapeDtypeStruct((m, d), out_dtype),
        compiler_params=_params(6 * tm * d * 4, 1),
        name="rmsnorm",
    )(x, g.reshape(1, d))


def _first_m_step():
    return pl.program_id(1) == 0


def _m_tile(n, m, m_tiles):
    return jnp.where(n % 2 == 0, m, m_tiles - 1 - m)


def _a_spec(tm, width, m_tiles):
    return pl.BlockSpec((tm, width), lambda n, m: (_m_tile(n, m, m_tiles), 0))


def _o_spec(tm, tn, m_tiles):
    return pl.BlockSpec((tm, tn), lambda n, m: (_m_tile(n, m, m_tiles), n))


def _weight_spec(k, tn, layer, col_block0=0, single_buffer=False):
    mode = pl.Buffered(1) if single_buffer else None
    return pl.BlockSpec((None, k, tn), lambda n, m: (layer, 0, col_block0 + n), pipeline_mode=mode)


def _proj_kernel(a_ref, as_ref, w_ref, o_ref, os_ref, wb_ref):
    @pl.when(_first_m_step())
    def _():
        wb_ref[...] = w_ref[...].astype(BF16)
        os_ref[...] = _dot(as_ref[...], wb_ref[...])

    o_ref[...] = _dot(a_ref[...], wb_ref[...])


def _proj(a, a_s, w, layer, n_cols, tm, tn):
    m, k = a.shape
    rows_s = a_s.shape[0]
    assert m % tm == 0 and n_cols % tn == 0
    return pl.pallas_call(
        _proj_kernel,
        grid=(n_cols // tn, m // tm),
        in_specs=[_a_spec(tm, k, m // tm),
                  pl.BlockSpec((rows_s, k), lambda n, m: (0, 0)),
                  _weight_spec(k, tn, layer)],
        out_specs=[_o_spec(tm, tn, m // tm),
                   pl.BlockSpec((rows_s, tn), lambda n, m: (0, n))],
        out_shape=[jax.ShapeDtypeStruct((m, n_cols), F32),
                   jax.ShapeDtypeStruct((rows_s, n_cols), F32)],
        scratch_shapes=[pltpu.VMEM((k, tn), BF16)],
        compiler_params=_params(2 * (tm * k * 2 + k * tn * 4 + tm * tn * 4) + k * tn * 2 + 8 * MIB, 2),
        name="in_proj",
    )(a, a_s, w)


def _kv_proj_kernel(*refs, layer, heads_per_step, tiles_per_seq, n_tiles, m_tiles):
    a_ref, as_ref, w_ref = refs[:3]
    o5_ref, o2_ref, os_ref, wb_ref, stage_ref, sems = refs[-6:]
    tm = a_ref.shape[0]
    step = pl.program_id(0) * m_tiles + pl.program_id(1)

    def row_copies(at_step):
        n, m = at_step // m_tiles, at_step % m_tiles
        tile = _m_tile(n, m, m_tiles)
        b, row0 = tile // tiles_per_seq, (tile % tiles_per_seq) * tm
        slot = at_step % 2
        return [pltpu.make_async_copy(
            stage_ref.at[slot, :, pl.ds(h * HEAD_DIM, HEAD_DIM)],
            o5_ref.at[layer, b, pl.ds(row0, tm), n * heads_per_step + h, :],
            sems.at[slot]) for h in range(heads_per_step)]

    @pl.when(_first_m_step())
    def _():
        wb_ref[...] = w_ref[...].astype(BF16)
        os_ref[...] = _dot(as_ref[...], wb_ref[...])

    acc = _dot(a_ref[...], wb_ref[...])
    o2_ref[...] = acc.astype(o2_ref.dtype)

    @pl.when(step >= 2)
    def _():
        for c in row_copies(step - 2):
            c.wait()

    stage_ref[step % 2] = acc
    for c in row_copies(step):
        c.start()

    @pl.when(step == n_tiles * m_tiles - 1)
    def _():
        if n_tiles * m_tiles >= 2:
            for c in row_copies(step - 1):
                c.wait()
        for c in row_copies(step):
            c.wait()


def _kv_proj(a, a_s, w, layer, col0, rows5, prev, tm):
    m, k = a.shape
    rows_s = a_s.shape[0]
    depth, batch, seq, n_heads, hd = rows5
    heads_per_step = 8
    tn = heads_per_step * hd
    assert seq % tm == 0 and n_heads % heads_per_step == 0 and col0 % tn == 0
    tiles_per_seq = seq // tm
    m_tiles = m // tm
    in_specs = [_a_spec(tm, k, m // tm),
                pl.BlockSpec((rows_s, k), lambda n, m: (0, 0)),
                _weight_spec(k, tn, layer, col0 // tn, single_buffer=True)]
    args = [a, a_s, w]
    aliases = {}
    if prev is not None:
        in_specs.append(pl.BlockSpec(memory_space=pl.ANY))
        args.append(prev)
        aliases = {3: 0}
    n_tiles = n_heads // heads_per_step
    return pl.pallas_call(
        functools.partial(_kv_proj_kernel, layer=layer, heads_per_step=heads_per_step,
                          tiles_per_seq=tiles_per_seq, n_tiles=n_tiles, m_tiles=m_tiles),
        grid=(n_tiles, m_tiles),
        in_specs=in_specs,
        out_specs=[pl.BlockSpec(memory_space=pl.ANY),
                   _o_spec(tm, tn, m_tiles),
                   pl.BlockSpec((rows_s, tn), lambda n, m: (0, n))],
        out_shape=[jax.ShapeDtypeStruct(rows5, F32),
                   jax.ShapeDtypeStruct((m, n_heads * hd), BF16),
                   jax.ShapeDtypeStruct((rows_s, n_heads * hd), F32)],
        scratch_shapes=[pltpu.VMEM((k, tn), BF16), pltpu.VMEM((2, tm, tn), F32),
                        pltpu.SemaphoreType.DMA((2,))],
        input_output_aliases=aliases,
        compiler_params=_params(
            2 * (tm * k * 2 + tm * tn * 6) + k * tn * 6 + 8 * MIB, 2),
        name="kv_proj",
    )(*args)


def _outproj_kernel(a1_ref, a2_ref, a1s_ref, a2s_ref, w_ref, r_ref, rs_ref, o_ref, os_ref, wb_ref):
    k1 = a1_ref.shape[1]

    def mix(x1_ref, x2_ref, res_ref):
        return res_ref[...] + (_dot(x1_ref[...], wb_ref[:k1, :]) + _dot(x2_ref[...], wb_ref[k1:, :]))

    @pl.when(_first_m_step())
    def _():
        wb_ref[...] = w_ref[...].astype(BF16)
        os_ref[...] = mix(a1s_ref, a2s_ref, rs_ref)

    o_ref[...] = mix(a1_ref, a2_ref, r_ref)


def _outproj(a1, a2, a1_s, a2_s, w, layer, res, res_s, tm, tn):
    m, k1 = a1.shape
    k2 = a2.shape[1]
    rows_s = a1_s.shape[0]
    k, n = k1 + k2, w.shape[2]
    assert m % tm == 0 and n % tn == 0
    return pl.pallas_call(
        _outproj_kernel,
        grid=(n // tn, m // tm),
        in_specs=[_a_spec(tm, k1, m // tm),
                  _a_spec(tm, k2, m // tm),
                  pl.BlockSpec((rows_s, k1), lambda n, m: (0, 0)),
                  pl.BlockSpec((rows_s, k2), lambda n, m: (0, 0)),
                  _weight_spec(k, tn, layer),
                  _o_spec(tm, tn, m // tm),
                  pl.BlockSpec((rows_s, tn), lambda n, m: (0, n))],
        out_specs=[_o_spec(tm, tn, m // tm),
                   pl.BlockSpec((rows_s, tn), lambda n, m: (0, n))],
        out_shape=[jax.ShapeDtypeStruct((m, n), F32),
                   jax.ShapeDtypeStruct((rows_s, n), F32)],
        scratch_shapes=[pltpu.VMEM((k, tn), BF16)],
        compiler_params=_params(
            2 * (tm * k * 2 + k * tn * 4 + 2 * tm * tn * 4) + k * tn * 2 + 8 * MIB, 2),
        name="out_proj",
    )(a1, a2, a1_s, a2_s, w, res, res_s)


def _gateup_kernel(h_ref, hs_ref, wg_ref, wu_ref, o_ref, os_ref, wgb_ref, wub_ref):
    def swiglu_in(x_ref):
        x = x_ref[...]
        return _silu(_dot(x, wgb_ref[...])) * _dot(x, wub_ref[...])

    @pl.when(_first_m_step())
    def _():
        wgb_ref[...] = wg_ref[...].astype(BF16)
        wub_ref[...] = wu_ref[...].astype(BF16)
        os_ref[...] = swiglu_in(hs_ref).astype(os_ref.dtype)

    o_ref[...] = swiglu_in(h_ref).astype(o_ref.dtype)


def _gateup(h, h_s, wg, wu, layer, tm, tn):
    m, k = h.shape
    rows_s = h_s.shape[0]
    n = wg.shape[2]
    assert m % tm == 0 and n % tn == 0
    return pl.pallas_call(
        _gateup_kernel,
        grid=(n // tn, m // tm),
        in_specs=[_a_spec(tm, k, m // tm),
                  pl.BlockSpec((rows_s, k), lambda n, m: (0, 0)),
                  _weight_spec(k, tn, layer), _weight_spec(k, tn, layer)],
        out_specs=[_o_spec(tm, tn, m // tm),
                   pl.BlockSpec((rows_s, tn), lambda n, m: (0, n))],
        out_shape=[jax.ShapeDtypeStruct((m, n), BF16),
                   jax.ShapeDtypeStruct((rows_s, n), BF16)],
        scratch_shapes=[pltpu.VMEM((k, tn), BF16), pltpu.VMEM((k, tn), BF16)],
        compiler_params=_params(
            2 * (tm * k * 2 + 2 * k * tn * 4 + tm * tn * 2) + 2 * k * tn * 2 + 8 * MIB, 2),
        name="ffn_gate_up",
    )(h, h_s, wg, wu)


def _down_kernel(a_ref, as_ref, w_ref, r_ref, rs_ref, o_ref, os_ref):
    @pl.when(_first_m_step())
    def _():
        os_ref[...] = rs_ref[...] + _dot(as_ref[...], w_ref[...])

    o_ref[...] = r_ref[...] + _dot(a_ref[...], w_ref[...])


def _down(a, a_s, w_bf16, layer, res, res_s, tm, tn):
    m, k = a.shape
    rows_s = a_s.shape[0]
    n = w_bf16.shape[2]
    assert m % tm == 0 and n % tn == 0
    return pl.pallas_call(
        _down_kernel,
        grid=(n // tn, m // tm),
        in_specs=[_a_spec(tm, k, m // tm),
                  pl.BlockSpec((rows_s, k), lambda n, m: (0, 0)),
                  _weight_spec(k, tn, layer),
                  _o_spec(tm, tn, m // tm),
                  pl.BlockSpec((rows_s, tn), lambda n, m: (0, n))],
        out_specs=[_o_spec(tm, tn, m // tm),
                   pl.BlockSpec((rows_s, tn), lambda n, m: (0, n))],
        out_shape=[jax.ShapeDtypeStruct((m, n), F32),
                   jax.ShapeDtypeStruct((rows_s, n), F32)],
        compiler_params=_params(2 * (tm * k * 2 + k * tn * 2 + 2 * tm * tn * 4) + 8 * MIB, 2),
        name="ffn_down",
    )(a, a_s, w_bf16, res, res_s)


def _rotary_tables(pos):
    half = HEAD_DIM // 2
    inv = ROPE_BASE ** (-jnp.arange(half, dtype=F32) / half)
    ang = pos.astype(F32)[:, None] * inv[None, :]
    cos, sin = jnp.cos(ang), jnp.sin(ang)
    return (jnp.concatenate([cos, cos], axis=-1), jnp.concatenate([-sin, sin], axis=-1))


def _log_decay(h_ret):
    return jnp.log1p(-jnp.exp2(-5.0 - jnp.arange(h_ret, dtype=F32)))


def _group_norm_gate(o, gn, g):
    mu = jnp.mean(o, axis=-1, keepdims=True)
    d = o - mu
    var = jnp.mean(d * d, axis=-1, keepdims=True)
    return _silu(g) * (d * lax.rsqrt(var + GN_EPS) * gn)


def _ret_prompt_kernel(q_ref, k_ref, v_ref, g_ref, cos_ref, sin_ref, intra_ref, qdec_ref,
                       kdec_ref, cdec_ref, gn_ref, o_ref, s_ref,
                       qb_ref, kb_ref, qd_ref, kd_ref, vb_ref, of_ref, *, chunk, n_chunks, unroll):
    half = HEAD_DIM // 2
    scale = HEAD_DIM ** -0.5
    seq = q_ref.shape[0]
    intra = intra_ref[0]
    cdec = cdec_ref[0, 0:1, :]

    cos = cos_ref[...]
    sin = sin_ref[...]
    q = q_ref[...]
    q = q * cos + pltpu.roll(q, half, 1) * sin
    k = k_ref[...]
    k = (k * cos + pltpu.roll(k, half, 1) * sin) * scale
    qb_ref[...] = q.astype(BF16)
    kb_ref[...] = k.astype(BF16)
    qd = q.reshape(n_chunks, chunk, HEAD_DIM) * qdec_ref[...]
    kd = k.reshape(n_chunks, chunk, HEAD_DIM) * kdec_ref[...]
    qd_ref[...] = qd.reshape(seq, HEAD_DIM).astype(BF16)
    kd_ref[...] = kd.reshape(seq, HEAD_DIM).astype(BF16)
    vb_ref[...] = v_ref[...].astype(BF16)

    def body(c, s):
        r = pl.ds(pl.multiple_of(c * chunk, chunk), chunk)
        v = vb_ref[r, :]
        sc = _dot_nt(qb_ref[r, :], kb_ref[r, :]) * intra
        of_ref[r, :] = _dot(sc.astype(BF16), v) + _dot(qd_ref[r, :], s.astype(BF16))
        return s * cdec + _dot_tn(kd_ref[r, :], v)

    s_fin = lax.fori_loop(0, n_chunks, body, jnp.zeros((HEAD_DIM, HEAD_DIM), F32),
                          unroll=unroll)
    s_ref[0, 0] = s_fin
    o_ref[...] = _group_norm_gate(of_ref[...], gn_ref[0], g_ref[...]).astype(o_ref.dtype)


def _ret_prompt(proj, gn_w, batch, seq, h_ret):
    chunk = math.gcd(seq, RET_CHUNK)
    n_chunks = seq // chunk
    cos2, sin2 = _rotary_tables(jnp.arange(seq))
    lg = _log_decay(h_ret)
    i = jnp.arange(chunk, dtype=F32)
    rel = i[:, None] - i[None, :]
    intra = jnp.exp(jnp.where(rel[None] >= 0, rel[None] * lg[:, None, None], -jnp.inf))
    ones = jnp.ones((1, 1, HEAD_DIM), F32)
    qdec = jnp.exp((i + 1.0)[None, :, None] * lg[:, None, None]) * ones
    kdec = jnp.exp((chunk - 1.0 - i)[None, :, None] * lg[:, None, None]) * ones
    cdec = jnp.exp(chunk * lg)[:, None, None] * jnp.ones((1, 8, HEAD_DIM), F32)
    gn = gn_w.astype(F32).reshape(h_ret, 1, HEAD_DIM)
    hd = HEAD_DIM

    def col(group):
        return pl.BlockSpec((seq, hd), lambda b, h: (b, group * h_ret + h))

    const = lambda shape: pl.BlockSpec(shape, lambda b, h: (h, 0, 0))
    unroll = math.gcd(n_chunks, 16)
    return pl.pallas_call(
        functools.partial(_ret_prompt_kernel, chunk=chunk, n_chunks=n_chunks, unroll=unroll),
        grid=(batch, h_ret),
        in_specs=[col(0), col(1), col(2), col(3),
                  pl.BlockSpec((seq, hd), lambda b, h: (0, 0)),
                  pl.BlockSpec((seq, hd), lambda b, h: (0, 0)),
                  const((1, chunk, chunk)), const((1, chunk, hd)), const((1, chunk, hd)),
                  const((1, 8, hd)), const((1, 1, hd))],
        out_specs=[pl.BlockSpec((seq, hd), lambda b, h: (b, h)),
                   pl.BlockSpec((1, 1, hd, hd), lambda b, h: (b, h, 0, 0))],
        out_shape=[jax.ShapeDtypeStruct((batch * seq, h_ret * hd), BF16),
                   jax.ShapeDtypeStruct((batch, h_ret, hd, hd), F32)],
        scratch_shapes=[pltpu.VMEM((seq, hd), BF16)] * 5 + [pltpu.VMEM((seq, hd), F32)],
        compiler_params=_params(2 * 7 * seq * hd * 4 + 16 * MIB, 2),
        name="retention_prompt",
    )(proj, proj, proj, proj, cos2, sin2, intra, qdec, kdec, cdec, gn)


def _block_keep(gate_t, n_past, eye_bf16):
    blk_id = lax.broadcasted_iota(jnp.int32, gate_t.shape, 0)
    g = jnp.where(blk_id < n_past, gate_t, NEG_INF)
    rank = jnp.zeros(gate_t.shape, F32)
    for i in range(n_past):
        gi = g[i:i + 1, :]
        rank = rank + jnp.where(gi > g, 1.0,
                                jnp.where(gi == g, jnp.where(blk_id > i, 1.0, 0.0), 0.0))
    keep_t = jnp.where(blk_id < n_past, jnp.where(rank < float(MOBA_TOPK), 1.0, 0.0), 0.0)
    pad = jnp.zeros((HEAD_DIM - gate_t.shape[0], gate_t.shape[1]), F32)
    keep_t = jnp.concatenate([keep_t, pad], axis=0).astype(BF16)
    return _dot_tn(keep_t, eye_bf16)


def _page_block_means(pages, o_ref, pages_per_block):
    n_out = len(pages) // pages_per_block
    n_heads = pages[0].shape[3]
    inv_n = 1.0 / (pages[0].shape[2] * pages_per_block)
    for i in range(n_out):
        acc = jnp.sum(pages[i * pages_per_block][0, 0], axis=0)
        for p in range(1, pages_per_block):
            acc = acc + jnp.sum(pages[i * pages_per_block + p][0, 0], axis=0)
        acc = acc * inv_n
        for h in range(n_heads):
            o_ref[0, h, i:i + 1, :] = acc[h:h + 1, :]


def _moba_prompt_kernel(pt_ref, q_ref, k_ref, v_ref, *refs, n_blocks, pages_per_block,
                        n_kmean_steps):
    del pt_ref
    pages, (o_ref, km_ref) = refs[:-2], refs[-2:]
    blk = MOBA_BLOCK
    scale = HEAD_DIM ** -0.5

    if n_kmean_steps is None:
        _page_block_means(pages, km_ref, pages_per_block)
    else:
        step = pl.program_id(0) * pl.num_programs(1) + pl.program_id(1)

        @pl.when(step < n_kmean_steps)
        def _():
            _page_block_means(pages, km_ref, pages_per_block)

    kb = k_ref[...]
    vb = v_ref[...]
    kmean = jnp.mean(kb.astype(F32).reshape(n_blocks, blk, HEAD_DIM), axis=1)
    n_rank_rows = -(-n_blocks // 8) * 8
    kmean_pad = jnp.concatenate(
        [kmean, jnp.zeros((HEAD_DIM - n_blocks, HEAD_DIM), F32)], axis=0)
    row = lax.broadcasted_iota(jnp.int32, (blk, blk), 0)
    colv = lax.broadcasted_iota(jnp.int32, (blk, blk), 1)
    causal = colv <= row
    r128 = lax.broadcasted_iota(jnp.int32, (HEAD_DIM, HEAD_DIM), 0)
    c128 = lax.broadcasted_iota(jnp.int32, (HEAD_DIM, HEAD_DIM), 1)
    eye_bf16 = jnp.where(r128 == c128, 1.0, 0.0).astype(BF16)

    for t in range(n_blocks):
        rows = slice(t * blk, (t + 1) * blk)
        qf = q_ref[rows, :]
        s = _dot_nt((qf * scale).astype(BF16), kb[:(t + 1) * blk])
        pieces = []
        if t > MOBA_TOPK:
            gate_t = _dot_nt(kmean_pad, qf, precision=lax.Precision.HIGHEST)
            keep = _block_keep(gate_t[:n_rank_rows], t, eye_bf16)
        for j in range(t):
            sj = s[:, j * blk:(j + 1) * blk]
            if t > MOBA_TOPK:
                sj = jnp.where(keep[:, j:j + 1] > 0.5, sj, NEG_INF)
            pieces.append(sj)
        pieces.append(jnp.where(causal, s[:, t * blk:], NEG_INF))
        s = jnp.concatenate(pieces, axis=1) if len(pieces) > 1 else pieces[0]
        m = jnp.max(s, axis=1, keepdims=True)
        p = jnp.exp(s - m)
        l = jnp.sum(p, axis=1, keepdims=True)
        o = _dot(p.astype(BF16), vb[:(t + 1) * blk]) / l
        o_ref[rows, :] = o.astype(o_ref.dtype)


def _moba_prompt(proj, k2d, v2d, batch, seq, h_att, q_col0, cache, layer, page_table,
                 pages_per_block):
    assert seq % MOBA_BLOCK == 0
    n_blocks = seq // MOBA_BLOCK
    assert n_blocks <= HEAD_DIM
    hd = HEAD_DIM
    db, n_pages = page_table.shape
    page, n_heads = cache.shape[2:4]
    pps = KMEAN_PAGES_PER_STEP
    assert n_pages % pps == 0 and pps % pages_per_block == 0
    blocks_per_step = pps // pages_per_block
    assert blocks_per_step % 8 == 0
    chunks = n_pages // pps
    n_kmean_steps = db * chunks
    assert n_kmean_steps <= batch * h_att, "not enough attention steps to carry the page stream"

    def kmean_pos(b, h):
        s = jnp.minimum(b * h_att + h, n_kmean_steps - 1)
        return s // chunks, s % chunks

    def page_spec(i):
        def index(b, h, pt):
            kb, kc = kmean_pos(b, h)
            return layer, pt[kb, kc * pps + i], 0, 0, 0
        return pl.BlockSpec((1, 1, page, n_heads, hd), index)

    def kmean_index(b, h, pt):
        kb, kc = kmean_pos(b, h)
        return kb, 0, kc, 0

    head = pl.BlockSpec((seq, hd), lambda b, h, pt: (b, h))
    grid_spec = pltpu.PrefetchScalarGridSpec(
        num_scalar_prefetch=1,
        grid=(batch, h_att),
        in_specs=[pl.BlockSpec((seq, hd), lambda b, h, pt: (b, q_col0 + h)), head, head]
        + [page_spec(i) for i in range(pps)],
        out_specs=[head, pl.BlockSpec((1, n_heads, blocks_per_step, hd), kmean_index)],
    )
    return pl.pallas_call(
        functools.partial(_moba_prompt_kernel, n_blocks=n_blocks, pages_per_block=pages_per_block,
                          n_kmean_steps=None if n_kmean_steps == batch * h_att else n_kmean_steps),
        grid_spec=grid_spec,
        out_shape=[jax.ShapeDtypeStruct((batch * seq, h_att * hd), BF16),
                   jax.ShapeDtypeStruct((db, n_heads, n_pages // pages_per_block, hd), F32)],
        compiler_params=_params(2 * pps * page * n_heads * hd * 4 + 24 * MIB, 2),
        name="moba_prompt",
    )(page_table, proj, k2d, v2d, *([cache] * pps))


def _row_to_col(x_row, eye):
    n = x_row.shape[1]
    return jnp.sum(jnp.where(eye, jnp.broadcast_to(x_row, (n, n)), 0.0), axis=1, keepdims=True)


def _ret_sample_kernel(q_ref, k_ref, v_ref, g_ref, s_ref, cos_ref, sin_ref, dec_ref, gn_ref,
                       o_ref, so_ref, *, h_ret):
    half = HEAD_DIM // 2
    scale = HEAD_DIM ** -0.5
    cos = cos_ref[...]
    sin = sin_ref[...]
    q = q_ref[0]
    q = q * cos + pltpu.roll(q, half, 1) * sin
    k = k_ref[0]
    k = (k * cos + pltpu.roll(k, half, 1) * sin) * scale
    v = v_ref[0]
    r0 = lax.broadcasted_iota(jnp.int32, (HEAD_DIM, HEAD_DIM), 0)
    r1 = lax.broadcasted_iota(jnp.int32, (HEAD_DIM, HEAD_DIM), 1)
    eye = r0 == r1
    for h in range(h_ret):
        hh = slice(h, h + 1)
        kcol = _row_to_col(k[hh, :], eye)
        qcol = _row_to_col(q[hh, :], eye)
        s_new = s_ref[0, 0, h] * dec_ref[hh, :] + kcol * v[hh, :]
        so_ref[0, 0, h] = s_new
        o = jnp.sum(qcol * s_new, axis=0, keepdims=True)
        o_ref[0, hh, :] = _group_norm_gate(o, gn_ref[hh, :], g_ref[0, hh, :]).astype(o_ref.dtype)


def _ret_sample(proj3, state, layer, gn_w, pos, dec_batch, h_ret):
    hd = HEAD_DIM
    cos2, sin2 = _rotary_tables(pos)
    dec = jnp.exp(1.0 * _log_decay(h_ret))[:, None] * jnp.ones((1, hd), F32)
    gn = gn_w.astype(F32).reshape(h_ret, hd)
    grp = lambda g: pl.BlockSpec((1, h_ret, hd), lambda b: (b, g, 0))
    full = lambda shape: pl.BlockSpec(shape, lambda b: (0,) * len(shape))
    return pl.pallas_call(
        functools.partial(_ret_sample_kernel, h_ret=h_ret),
        grid=(dec_batch,),
        in_specs=[grp(0), grp(1), grp(2), grp(3),
                  pl.BlockSpec((1, 1, h_ret, hd, hd), lambda b: (layer, b, 0, 0, 0)),
                  full((1, hd)), full((1, hd)), full((h_ret, hd)), full((h_ret, hd))],
        out_specs=[pl.BlockSpec((1, h_ret, hd), lambda b: (b, 0, 0)),
                   pl.BlockSpec((1, 1, h_ret, hd, hd), lambda b: (0, b, 0, 0, 0))],
        out_shape=[jax.ShapeDtypeStruct((dec_batch, h_ret, hd), F32),
                   jax.ShapeDtypeStruct((1, dec_batch, h_ret, hd, hd), F32)],
        compiler_params=_params(16 * MIB, 1),
        name="retention_sample",
    )(proj3, proj3, proj3, proj3, state, cos2, sin2, dec, gn)


def _select_kernel(km_ref, q_ref, o_ref, *, n_sel):
    n_heads, nb = km_ref.shape[1], km_ref.shape[2]
    blk_id = lax.broadcasted_iota(jnp.int32, (nb, 1), 0).astype(F32)
    row = lax.broadcasted_iota(jnp.int32, o_ref.shape[1:], 0)
    lane = lax.broadcasted_iota(jnp.int32, o_ref.shape[1:], 1)
    out = jnp.zeros(o_ref.shape[1:], jnp.int32)
    for h in range(n_heads):
        gate = jnp.sum(km_ref[0, h] * q_ref[0, h:h + 1, :], axis=1, keepdims=True)
        for r in range(n_sel):
            best = jnp.max(gate, axis=0, keepdims=True)
            idx = jnp.min(jnp.where(gate == best, blk_id, float(nb)), axis=0, keepdims=True)
            out = jnp.where((row == r) & (lane == h), idx.astype(jnp.int32), out)
            gate = jnp.where(blk_id == idx, NEG_INF, gate)
    o_ref[0] = out


def _select_sample(kmean, proj3, q_group, n_sel):
    db, n_heads, nb, hd = kmean.shape
    assert n_sel <= 8 and n_heads <= hd
    return pl.pallas_call(
        functools.partial(_select_kernel, n_sel=n_sel),
        grid=(db,),
        in_specs=[pl.BlockSpec((1, n_heads, nb, hd), lambda b: (b, 0, 0, 0)),
                  pl.BlockSpec((1, n_heads, hd), lambda b: (b, q_group, 0))],
        out_specs=pl.BlockSpec((1, 8, hd), lambda b: (b, 0, 0)),
        out_shape=jax.ShapeDtypeStruct((db, 8, hd), jnp.int32),
        compiler_params=_params(16 * MIB, 1),
        name="moba_sample_select",
    )(kmean, proj3)


def _attn_sample_kernel(pt_ref, sel_ref, q_ref, kn_ref, vn_ref, ck_ref, cv_ref, o_ref,
                        kbuf, vbuf, sems, *, layer, n_sel, pages_per_block):
    b = pl.program_id(0)
    n_heads, n_pages_sel, page, hd = kbuf.shape
    scale = hd ** -0.5

    def head_copies(h):
        copies = []
        for r in range(n_sel):
            blk = sel_ref[b * 8 + r, h]
            for i in range(pages_per_block):
                phys = pt_ref[b, blk * pages_per_block + i]
                slot = r * pages_per_block + i
                copies.append(pltpu.make_async_copy(
                    ck_ref.at[layer, phys, :, h, :], kbuf.at[h, slot], sems.at[0, h]))
                copies.append(pltpu.make_async_copy(
                    cv_ref.at[layer, phys, :, h, :], vbuf.at[h, slot], sems.at[1, h]))
        return copies

    for h in range(n_heads):
        for c in head_copies(h):
            c.start()

    for h in range(n_heads):
        for c in head_copies(h):
            c.wait()
        hh = slice(h, h + 1)
        q = q_ref[0, hh, :]
        k_all = kbuf[h].reshape(n_pages_sel * page, hd).astype(BF16)
        v_all = vbuf[h].reshape(n_pages_sel * page, hd).astype(BF16)
        q8 = jnp.broadcast_to(q * scale, (8, hd)).astype(BF16)
        s = _dot_nt(q8, k_all)
        s_own = jnp.sum(kn_ref[0, hh, :] * q, axis=1, keepdims=True) * scale
        m = jnp.maximum(jnp.max(s, axis=1, keepdims=True), s_own)
        e = jnp.exp(s - m)
        e_own = jnp.exp(s_own - m)
        l = jnp.sum(e, axis=1, keepdims=True) + e_own
        o = _dot(e.astype(BF16), v_all) + e_own * vn_ref[0, hh, :]
        o_ref[0, hh, :] = (o / l)[0:1, :]


def _attn_sample(cache_k, cache_v, layer, page_table, sel2, q3, q_group, kn3, vn3,
                 dec_batch, n_sel, pages_per_block):
    page, h_att, hd = cache_k.shape[2:]
    n_pages_sel = n_sel * pages_per_block
    grid_spec = pltpu.PrefetchScalarGridSpec(
        num_scalar_prefetch=2,
        grid=(dec_batch,),
        in_specs=[pl.BlockSpec((1, h_att, hd), lambda b, pt, sel: (b, q_group, 0)),
                  pl.BlockSpec((1, h_att, hd), lambda b, pt, sel: (b, 0, 0)),
                  pl.BlockSpec((1, h_att, hd), lambda b, pt, sel: (b, 0, 0)),
                  pl.BlockSpec(memory_space=pl.ANY),
                  pl.BlockSpec(memory_space=pl.ANY)],
        out_specs=pl.BlockSpec((1, h_att, hd), lambda b, pt, sel: (b, 0, 0)),
        scratch_shapes=[pltpu.VMEM((h_att, n_pages_sel, page, hd), F32),
                        pltpu.VMEM((h_att, n_pages_sel, page, hd), F32),
                        pltpu.SemaphoreType.DMA((2, h_att))],
    )
    return pl.pallas_call(
        functools.partial(_attn_sample_kernel, layer=layer, n_sel=n_sel,
                          pages_per_block=pages_per_block),
        grid_spec=grid_spec,
        out_shape=jax.ShapeDtypeStruct((dec_batch, h_att, hd), F32),
        compiler_params=_params(2 * h_att * n_pages_sel * page * hd * 4 + 16 * MIB, 1),
        name="moba_sample_attn",
    )(page_table, sel2, q3, kn3, vn3, cache_k, cache_v)


def _pad_rows(x, rows):
    return jnp.pad(x, ((0, rows - x.shape[0]), (0, 0)))


def kernel(x_prompt, x_sample, cache_k, cache_v, state_ret, page_table, w_in, w_out, norm_mix,
           ret_gn, norm_ffn, w_gate, w_up, w_down, norm_final):
    batch, seq, d_model = x_prompt.shape
    dec_batch, dec_seq, _ = x_sample.shape
    depth, n_pool, page_size, h_att, hd = cache_k.shape
    h_ret = state_ret.shape[2]
    w_ret, w_att = h_ret * hd, h_att * hd
    n_pages = page_table.shape[1]
    past_len = n_pages * page_size
    assert hd == HEAD_DIM and h_ret == h_att and dec_seq == 1
    assert MOBA_BLOCK % page_size == 0 and past_len % MOBA_BLOCK == 0
    pages_per_block = MOBA_BLOCK // page_size
    n_past_blocks = past_len // MOBA_BLOCK
    assert n_past_blocks >= MOBA_TOPK and dec_batch <= SAMPLE_ROWS
    n_groups = 7
    att_group = 4
    in_cols = w_in.shape[2]
    assert in_cols == n_groups * w_ret

    m_p = batch * seq
    yp = x_prompt.reshape(m_p, d_model)
    ys = _pad_rows(x_sample.reshape(dec_batch, d_model), SAMPLE_ROWS)
    pos_s = past_len + jnp.arange(dec_seq)

    w_down_b = w_down.astype(BF16)
    n_mix = 5

    rp_states, ks_rows, vs_rows, rs_states = [], [], [], []
    rows5 = (depth, batch, seq, h_att, hd)
    k_prompt = v_prompt = None
    for l in range(depth):
        hp = _rmsnorm(yp, norm_mix[l], BF16, 256)
        hs = _rmsnorm(ys, norm_mix[l], BF16, SAMPLE_ROWS)
        proj_p, proj_s = _proj(hp, hs, w_in, l, n_mix * w_ret, 1024, 512)
        k_prompt, k2d, k_s = _kv_proj(hp, hs, w_in, l, 5 * w_ret, rows5, k_prompt, 512)
        v_prompt, v2d, v_s = _kv_proj(hp, hs, w_in, l, 6 * w_ret, rows5, v_prompt, 512)

        ret_o, ret_s = _ret_prompt(proj_p, ret_gn[l], batch, seq, h_ret)
        att_o, kmean = _moba_prompt(proj_p, k2d, v2d, batch, seq, h_att, att_group * h_ret,
                                    cache_k, l, page_table, pages_per_block)
        rp_states.append(ret_s)

        proj3 = proj_s.reshape(SAMPLE_ROWS, n_mix * h_ret, hd)
        kn3 = k_s.reshape(SAMPLE_ROWS, h_att, hd)
        vn3 = v_s.reshape(SAMPLE_ROWS, h_att, hd)
        ret_os, ret_ss = _ret_sample(proj3, state_ret, l, ret_gn[l], pos_s, dec_batch, h_ret)
        sel = _select_sample(kmean, proj3, att_group, MOBA_TOPK)
        att_os = _attn_sample(cache_k, cache_v, l, page_table, sel.reshape(dec_batch * 8, hd),
                              proj3, att_group, kn3, vn3, dec_batch, MOBA_TOPK, pages_per_block)
        ks_rows.append(k_s[:dec_batch].reshape(dec_batch, 1, h_att, hd))
        vs_rows.append(v_s[:dec_batch].reshape(dec_batch, 1, h_att, hd))
        rs_states.append(ret_ss[0])
        mix_r = _pad_rows(ret_os.reshape(dec_batch, w_ret), SAMPLE_ROWS).astype(BF16)
        mix_a = _pad_rows(att_os.reshape(dec_batch, w_att), SAMPLE_ROWS).astype(BF16)

        yp, ys = _outproj(ret_o, att_o, mix_r, mix_a, w_out, l, yp, ys, 1024, 512)
        hf = _rmsnorm(yp, norm_ffn[l], BF16, 256)
        hfs = _rmsnorm(ys, norm_ffn[l], BF16, SAMPLE_ROWS)
        act, acts = _gateup(hf, hfs, w_gate, w_up, l, 1024, 256)
        yp, ys = _down(act, acts, w_down_b, l, yp, ys, 512, 512)

    y_prompt = _rmsnorm(yp, norm_final, F32, 256).reshape(batch, seq, d_model)
    y_sample = _rmsnorm(ys, norm_final, F32, SAMPLE_ROWS)[:dec_batch].reshape(dec_batch, 1, d_model)
    return (y_prompt, y_sample, k_prompt, v_prompt, jnp.stack(rp_states),
            jnp.stack(ks_rows), jnp.stack(vs_rows), jnp.stack(rs_states))
```

```python
import functools
import math

import jax
import jax.numpy as jnp
from jax import lax
from jax.experimental import pallas as pl
from jax.experimental.pallas import tpu as pltpu

HEAD_DIM = 128
RET_CHUNK = 128
MOBA_BLOCK = 256
MOBA_TOPK = 3
ROPE_BASE = 10000.0
NORM_EPS = 1e-6
GN_EPS = 1e-5

BF16 = jnp.bfloat16
F32 = jnp.float32
NEG_INF = float("-inf")
MIB = 1024 * 1024
V7X_VMEM_BUDGET = 56 * MIB
SAMPLE_ROWS = 16
KMEAN_PAGES_PER_STEP = 16


def _params(vmem_bytes, n_axes):
    return pltpu.CompilerParams(
        dimension_semantics=("arbitrary",) * n_axes,
        vmem_limit_bytes=int(min(max(vmem_bytes, 16 * MIB), V7X_VMEM_BUDGET)))


def _dot(a, b):
    return jnp.dot(a, b, preferred_element_type=F32)


def _dot_nt(a, b, precision=None):
    return lax.dot_general(a, b, (((1,), (1,)), ((), ())), precision=precision,
                           preferred_element_type=F32)


def _dot_tn(a, b):
    return lax.dot_general(a, b, (((0,), (0,)), ((), ())), preferred_element_type=F32)


def _silu(x):
    return x * (1.0 / (1.0 + jnp.exp(-x)))


def _rmsnorm_kernel(x_ref, g_ref, o_ref):
    x = x_ref[...]
    y = x * lax.rsqrt(jnp.mean(x * x, axis=-1, keepdims=True) + NORM_EPS)
    o_ref[...] = (y * g_ref[...]).astype(o_ref.dtype)


def _rmsnorm(x, g, out_dtype, tm):
    m, d = x.shape
    return pl.pallas_call(
        _rmsnorm_kernel,
        grid=(m // tm,),
        in_specs=[pl.BlockSpec((tm, d), lambda i: (i, 0)),
                  pl.BlockSpec((1, d), lambda i: (0, 0))],
        out_specs=pl.BlockSpec((tm, d), lambda i: (i, 0)),
        out_shape=jax.ShapeDtypeStruct((m, d), out_dtype),
        compiler_params=_params(6 * tm * d * 4, 1),
        name="rmsnorm",
    )(x, g.reshape(1, d))


def _first_m_step():
    return pl.program_id(1) == 0


def _m_tile(n, m, m_tiles):
    return jnp.where(n % 2 == 0, m, m_tiles - 1 - m)


def _a_spec(tm, width, m_tiles):
    return pl.BlockSpec((tm, width), lambda n, m: (_m_tile(n, m, m_tiles), 0))


def _o_spec(tm, tn, m_tiles):
    return pl.BlockSpec((tm, tn), lambda n, m: (_m_tile(n, m, m_tiles), n))


def _weight_spec(k, tn, layer, col_block0=0, single_buffer=False):
    mode = pl.Buffered(1) if single_buffer else None
    return pl.BlockSpec((None, k, tn), lambda n, m: (layer, 0, col_block0 + n), pipeline_mode=mode)


def _proj_kernel(a_ref, as_ref, w_ref, o_ref, os_ref, wb_ref):
    @pl.when(_first_m_step())
    def _():
        wb_ref[...] = w_ref[...].astype(BF16)
        os_ref[...] = _dot(as_ref[...], wb_ref[...])

    o_ref[...] = _dot(a_ref[...], wb_ref[...])


def _proj(a, a_s, w, layer, n_cols, tm, tn):
    m, k = a.shape
    rows_s = a_s.shape[0]
    assert m % tm == 0 and n_cols % tn == 0
    return pl.pallas_call(
        _proj_kernel,
        grid=(n_cols // tn, m // tm),
        in_specs=[_a_spec(tm, k, m // tm),
                  pl.BlockSpec((rows_s, k), lambda n, m: (0, 0)),
                  _weight_spec(k, tn, layer)],
        out_specs=[_o_spec(tm, tn, m // tm),
                   pl.BlockSpec((rows_s, tn), lambda n, m: (0, n))],
        out_shape=[jax.ShapeDtypeStruct((m, n_cols), F32),
                   jax.ShapeDtypeStruct((rows_s, n_cols), F32)],
        scratch_shapes=[pltpu.VMEM((k, tn), BF16)],
        compiler_params=_params(2 * (tm * k * 2 + k * tn * 4 + tm * tn * 4) + k * tn * 2 + 8 * MIB, 2),
        name="in_proj",
    )(a, a_s, w)


def _kv_proj_kernel(*refs, layer, heads_per_step, tiles_per_seq, n_tiles, m_tiles):
    a_ref, as_ref, w_ref = refs[:3]
    o5_ref, o2_ref, os_ref, wb_ref, stage_ref, sems = refs[-6:]
    tm = a_ref.shape[0]
    step = pl.program_id(0) * m_tiles + pl.program_id(1)

    def row_copies(at_step):
        n, m = at_step // m_tiles, at_step % m_tiles
        tile = _m_tile(n, m, m_tiles)
        b, row0 = tile // tiles_per_seq, (tile % tiles_per_seq) * tm
        slot = at_step % 2
        return [pltpu.make_async_copy(
            stage_ref.at[slot, :, pl.ds(h * HEAD_DIM, HEAD_DIM)],
            o5_ref.at[layer, b, pl.ds(row0, tm), n * heads_per_step + h, :],
            sems.at[slot]) for h in range(heads_per_step)]

    @pl.when(_first_m_step())
    def _():
        wb_ref[...] = w_ref[...].astype(BF16)
        os_ref[...] = _dot(as_ref[...], wb_ref[...])

    acc = _dot(a_ref[...], wb_ref[...])
    o2_ref[...] = acc.astype(o2_ref.dtype)

    @pl.when(step >= 2)
    def _():
        for c in row_copies(step - 2):
            c.wait()

    stage_ref[step % 2] = acc
    for c in row_copies(step):
        c.start()

    @pl.when(step == n_tiles * m_tiles - 1)
    def _():
        if n_tiles * m_tiles >= 2:
            for c in row_copies(step - 1):
                c.wait()
        for c in row_copies(step):
            c.wait()


def _kv_proj(a, a_s, w, layer, col0, rows5, prev, tm):
    m, k = a.shape
    rows_s = a_s.shape[0]
    depth, batch, seq, n_heads, hd = rows5
    heads_per_step = 8
    tn = heads_per_step * hd
    assert seq % tm == 0 and n_heads % heads_per_step == 0 and col0 % tn == 0
    tiles_per_seq = seq // tm
    m_tiles = m // tm
    in_specs = [_a_spec(tm, k, m // tm),
                pl.BlockSpec((rows_s, k), lambda n, m: (0, 0)),
                _weight_spec(k, tn, layer, col0 // tn, single_buffer=True)]
    args = [a, a_s, w]
    aliases = {}
    if prev is not None:
        in_specs.append(pl.BlockSpec(memory_space=pl.ANY))
        args.append(prev)
        aliases = {3: 0}
    n_tiles = n_heads // heads_per_step
    return pl.pallas_call(
        functools.partial(_kv_proj_kernel, layer=layer, heads_per_step=heads_per_step,
                          tiles_per_seq=tiles_per_seq, n_tiles=n_tiles, m_tiles=m_tiles),
        grid=(n_tiles, m_tiles),
        in_specs=in_specs,
        out_specs=[pl.BlockSpec(memory_space=pl.ANY),
                   _o_spec(tm, tn, m_tiles),
                   pl.BlockSpec((rows_s, tn), lambda n, m: (0, n))],
        out_shape=[jax.ShapeDtypeStruct(rows5, F32),
                   jax.ShapeDtypeStruct((m, n_heads * hd), BF16),
                   jax.ShapeDtypeStruct((rows_s, n_heads * hd), F32)],
        scratch_shapes=[pltpu.VMEM((k, tn), BF16), pltpu.VMEM((2, tm, tn), F32),
                        pltpu.SemaphoreType.DMA((2,))],
        input_output_aliases=aliases,
        compiler_params=_params(
            2 * (tm * k * 2 + tm * tn * 6) + k * tn * 6 + 8 * MIB, 2),
        name="kv_proj",
    )(*args)


def _outproj_kernel(a1_ref, a2_ref, a1s_ref, a2s_ref, w_ref, r_ref, rs_ref, o_ref, os_ref, wb_ref):
    k1 = a1_ref.shape[1]

    def mix(x1_ref, x2_ref, res_ref):
        return res_ref[...] + (_dot(x1_ref[...], wb_ref[:k1, :]) + _dot(x2_ref[...], wb_ref[k1:, :]))

    @pl.when(_first_m_step())
    def _():
        wb_ref[...] = w_ref[...].astype(BF16)
        os_ref[...] = mix(a1s_ref, a2s_ref, rs_ref)

    o_ref[...] = mix(a1_ref, a2_ref, r_ref)


def _outproj(a1, a2, a1_s, a2_s, w, layer, res, res_s, tm, tn):
    m, k1 = a1.shape
    k2 = a2.shape[1]
    rows_s = a1_s.shape[0]
    k, n = k1 + k2, w.shape[2]
    assert m % tm == 0 and n % tn == 0
    return pl.pallas_call(
        _outproj_kernel,
        grid=(n // tn, m // tm),
        in_specs=[_a_spec(tm, k1, m // tm),
                  _a_spec(tm, k2, m // tm),
                  pl.BlockSpec((rows_s, k1), lambda n, m: (0, 0)),
                  pl.BlockSpec((rows_s, k2), lambda n, m: (0, 0)),
                  _weight_spec(k, tn, layer),
                  _o_spec(tm, tn, m // tm),
                  pl.BlockSpec((rows_s, tn), lambda n, m: (0, n))],
        out_specs=[_o_spec(tm, tn, m // tm),
                   pl.BlockSpec((rows_s, tn), lambda n, m: (0, n))],
        out_shape=[jax.ShapeDtypeStruct((m, n), F32),
                   jax.ShapeDtypeStruct((rows_s, n), F32)],
        scratch_shapes=[pltpu.VMEM((k, tn), BF16)],
        compiler_params=_params(
            2 * (tm * k * 2 + k * tn * 4 + 2 * tm * tn * 4) + k * tn * 2 + 8 * MIB, 2),
        name="out_proj",
    )(a1, a2, a1_s, a2_s, w, res, res_s)


def _gateup_kernel(h_ref, hs_ref, wg_ref, wu_ref, o_ref, os_ref, wgb_ref, wub_ref):
    def swiglu_in(x_ref):
        x = x_ref[...]
        return _silu(_dot(x, wgb_ref[...])) * _dot(x, wub_ref[...])

    @pl.when(_first_m_step())
    def _():
        wgb_ref[...] = wg_ref[...].astype(BF16)
        wub_ref[...] = wu_ref[...].astype(BF16)
        os_ref[...] = swiglu_in(hs_ref).astype(os_ref.dtype)

    o_ref[...] = swiglu_in(h_ref).astype(o_ref.dtype)


def _gateup(h, h_s, wg, wu, layer, tm, tn):
    m, k = h.shape
    rows_s = h_s.shape[0]
    n = wg.shape[2]
    assert m % tm == 0 and n % tn == 0
    return pl.pallas_call(
        _gateup_kernel,
        grid=(n // tn, m // tm),
        in_specs=[_a_spec(tm, k, m // tm),
                  pl.BlockSpec((rows_s, k), lambda n, m: (0, 0)),
                  _weight_spec(k, tn, layer), _weight_spec(k, tn, layer)],
        out_specs=[_o_spec(tm, tn, m // tm),
                   pl.BlockSpec((rows_s, tn), lambda n, m: (0, n))],
        out_shape=[jax.ShapeDtypeStruct((m, n), BF16),
                   jax.ShapeDtypeStruct((rows_s, n), BF16)],
        scratch_shapes=[pltpu.VMEM((k, tn), BF16), pltpu.VMEM((k, tn), BF16)],
        compiler_params=_params(
            2 * (tm * k * 2 + 2 * k * tn * 4 + tm * tn * 2) + 2 * k * tn * 2 + 8 * MIB, 2),
        name="ffn_gate_up",
    )(h, h_s, wg, wu)


def _down_kernel(a_ref, as_ref, w_ref, r_ref, rs_ref, o_ref, os_ref):
    @pl.when(_first_m_step())
    def _():
        os_ref[...] = rs_ref[...] + _dot(as_ref[...], w_ref[...])

    o_ref[...] = r_ref[...] + _dot(a_ref[...], w_ref[...])


def _down(a, a_s, w_bf16, layer, res, res_s, tm, tn):
    m, k = a.shape
    rows_s = a_s.shape[0]
    n = w_bf16.shape[2]
    assert m % tm == 0 and n % tn == 0
    return pl.pallas_call(
        _down_kernel,
        grid=(n // tn, m // tm),
        in_specs=[_a_spec(tm, k, m // tm),
                  pl.BlockSpec((rows_s, k), lambda n, m: (0, 0)),
                  _weight_spec(k, tn, layer),
                  _o_spec(tm, tn, m // tm),
                  pl.BlockSpec((rows_s, tn), lambda n, m: (0, n))],
        out_specs=[_o_spec(tm, tn, m // tm),
                   pl.BlockSpec((rows_s, tn), lambda n, m: (0, n))],
        out_shape=[jax.ShapeDtypeStruct((m, n), F32),
                   jax.ShapeDtypeStruct((rows_s, n), F32)],
        compiler_params=_params(2 * (tm * k * 2 + k * tn * 2 + 2 * tm * tn * 4) + 8 * MIB, 2),
        name="ffn_down",
    )(a, a_s, w_bf16, res, res_s)


def _rotary_tables(pos):
    half = HEAD_DIM // 2
    inv = ROPE_BASE ** (-jnp.arange(half, dtype=F32) / half)
    ang = pos.astype(F32)[:, None] * inv[None, :]
    cos, sin = jnp.cos(ang), jnp.sin(ang)
    return (jnp.concatenate([cos, cos], axis=-1), jnp.concatenate([-sin, sin], axis=-1))


def _log_decay(h_ret):
    return jnp.log1p(-jnp.exp2(-5.0 - jnp.arange(h_ret, dtype=F32)))


def _group_norm_gate(o, gn, g):
    mu = jnp.mean(o, axis=-1, keepdims=True)
    d = o - mu
    var = jnp.mean(d * d, axis=-1, keepdims=True)
    return _silu(g) * (d * lax.rsqrt(var + GN_EPS) * gn)


def _ret_prompt_kernel(q_ref, k_ref, v_ref, g_ref, cos_ref, sin_ref, intra_ref, qdec_ref,
                       kdec_ref, cdec_ref, gn_ref, w_ref, o_ref, s_ref, wb_ref,
                       qb_ref, kb_ref, qd_ref, kd_ref, vb_ref, of_ref,
                       *, chunk, n_chunks, unroll, n_cast_steps):
    half = HEAD_DIM // 2
    scale = HEAD_DIM ** -0.5
    seq = q_ref.shape[0]
    intra = intra_ref[0]
    cdec = cdec_ref[0, 0:1, :]

    step = pl.program_id(0) * pl.num_programs(1) + pl.program_id(1)

    @pl.when(step < n_cast_steps)
    def _():
        wb_ref[...] = w_ref[...].astype(BF16)

    cos = cos_ref[...]
    sin = sin_ref[...]
    q = q_ref[...]
    q = q * cos + pltpu.roll(q, half, 1) * sin
    k = k_ref[...]
    k = (k * cos + pltpu.roll(k, half, 1) * sin) * scale
    qb_ref[...] = q.astype(BF16)
    kb_ref[...] = k.astype(BF16)
    qd = q.reshape(n_chunks, chunk, HEAD_DIM) * qdec_ref[...]
    kd = k.reshape(n_chunks, chunk, HEAD_DIM) * kdec_ref[...]
    qd_ref[...] = qd.reshape(seq, HEAD_DIM).astype(BF16)
    kd_ref[...] = kd.reshape(seq, HEAD_DIM).astype(BF16)
    vb_ref[...] = v_ref[...].astype(BF16)

    def body(c, s):
        r = pl.ds(pl.multiple_of(c * chunk, chunk), chunk)
        v = vb_ref[r, :]
        sc = _dot_nt(qb_ref[r, :], kb_ref[r, :]) * intra
        of_ref[r, :] = _dot(sc.astype(BF16), v) + _dot(qd_ref[r, :], s.astype(BF16))
        return s * cdec + _dot_tn(kd_ref[r, :], v)

    s_fin = lax.fori_loop(0, n_chunks, body, jnp.zeros((HEAD_DIM, HEAD_DIM), F32),
                          unroll=unroll)
    s_ref[0, 0] = s_fin
    o_ref[...] = _group_norm_gate(of_ref[...], gn_ref[0], g_ref[...]).astype(o_ref.dtype)


def _slab_rows(k, steps):
    for rows in range(16, k + 1, 16):
        if k % rows == 0 and k // rows <= steps:
            return rows
    raise ValueError(f"cannot split {k} rows into at most {steps} slabs")


def _ret_prompt(proj, gn_w, batch, seq, h_ret, w_stack, layer):
    w_k, w_n = w_stack.shape[1:]
    w_rows = _slab_rows(w_k, batch * h_ret)
    n_cast_steps = w_k // w_rows

    def w_index(b, h):
        return jnp.minimum(b * h_ret + h, n_cast_steps - 1), 0

    chunk = math.gcd(seq, RET_CHUNK)
    n_chunks = seq // chunk
    cos2, sin2 = _rotary_tables(jnp.arange(seq))
    lg = _log_decay(h_ret)
    i = jnp.arange(chunk, dtype=F32)
    rel = i[:, None] - i[None, :]
    intra = jnp.exp(jnp.where(rel[None] >= 0, rel[None] * lg[:, None, None], -jnp.inf))
    ones = jnp.ones((1, 1, HEAD_DIM), F32)
    qdec = jnp.exp((i + 1.0)[None, :, None] * lg[:, None, None]) * ones
    kdec = jnp.exp((chunk - 1.0 - i)[None, :, None] * lg[:, None, None]) * ones
    cdec = jnp.exp(chunk * lg)[:, None, None] * jnp.ones((1, 8, HEAD_DIM), F32)
    gn = gn_w.astype(F32).reshape(h_ret, 1, HEAD_DIM)
    hd = HEAD_DIM

    def col(group):
        return pl.BlockSpec((seq, hd), lambda b, h: (b, group * h_ret + h))

    const = lambda shape: pl.BlockSpec(shape, lambda b, h: (h, 0, 0))
    unroll = math.gcd(n_chunks, 16)
    return pl.pallas_call(
        functools.partial(_ret_prompt_kernel, chunk=chunk, n_chunks=n_chunks, unroll=unroll,
                          n_cast_steps=n_cast_steps),
        grid=(batch, h_ret),
        in_specs=[col(0), col(1), col(2), col(3),
                  pl.BlockSpec((seq, hd), lambda b, h: (0, 0)),
                  pl.BlockSpec((seq, hd), lambda b, h: (0, 0)),
                  const((1, chunk, chunk)), const((1, chunk, hd)), const((1, chunk, hd)),
                  const((1, 8, hd)), const((1, 1, hd)),
                  pl.BlockSpec((None, w_rows, w_n), lambda b, h: (layer,) + w_index(b, h))],
        out_specs=[pl.BlockSpec((seq, hd), lambda b, h: (b, h)),
                   pl.BlockSpec((1, 1, hd, hd), lambda b, h: (b, h, 0, 0)),
                   pl.BlockSpec((None, w_rows, w_n), lambda b, h: (0,) + w_index(b, h))],
        out_shape=[jax.ShapeDtypeStruct((batch * seq, h_ret * hd), BF16),
                   jax.ShapeDtypeStruct((batch, h_ret, hd, hd), F32),
                   jax.ShapeDtypeStruct((1, w_k, w_n), BF16)],
        scratch_shapes=[pltpu.VMEM((seq, hd), BF16)] * 5 + [pltpu.VMEM((seq, hd), F32)],
        compiler_params=_params(2 * 7 * seq * hd * 4 + 2 * w_rows * w_n * 6 + 16 * MIB, 2),
        name="retention_prompt",
    )(proj, proj, proj, proj, cos2, sin2, intra, qdec, kdec, cdec, gn, w_stack)


def _block_keep(gate_t, n_past, eye_bf16):
    blk_id = lax.broadcasted_iota(jnp.int32, gate_t.shape, 0)
    g = jnp.where(blk_id < n_past, gate_t, NEG_INF)
    rank = jnp.zeros(gate_t.shape, F32)
    for i in range(n_past):
        gi = g[i:i + 1, :]
        rank = rank + jnp.where(gi > g, 1.0,
                                jnp.where(gi == g, jnp.where(blk_id > i, 1.0, 0.0), 0.0))
    keep_t = jnp.where(blk_id < n_past, jnp.where(rank < float(MOBA_TOPK), 1.0, 0.0), 0.0)
    pad = jnp.zeros((HEAD_DIM - gate_t.shape[0], gate_t.shape[1]), F32)
    keep_t = jnp.concatenate([keep_t, pad], axis=0).astype(BF16)
    return _dot_tn(keep_t, eye_bf16)


def _page_block_means(pages, o_ref, pages_per_block):
    n_out = len(pages) // pages_per_block
    n_heads = pages[0].shape[3]
    inv_n = 1.0 / (pages[0].shape[2] * pages_per_block)
    for i in range(n_out):
        acc = jnp.sum(pages[i * pages_per_block][0, 0], axis=0)
        for p in range(1, pages_per_block):
            acc = acc + jnp.sum(pages[i * pages_per_block + p][0, 0], axis=0)
        acc = acc * inv_n
        for h in range(n_heads):
            o_ref[0, h, i:i + 1, :] = acc[h:h + 1, :]


def _moba_prompt_kernel(pt_ref, q_ref, k_ref, v_ref, *refs, n_blocks, pages_per_block,
                        n_kmean_steps):
    del pt_ref
    pages, (o_ref, km_ref) = refs[:-2], refs[-2:]
    blk = MOBA_BLOCK
    scale = HEAD_DIM ** -0.5

    if n_kmean_steps is None:
        _page_block_means(pages, km_ref, pages_per_block)
    else:
        step = pl.program_id(0) * pl.num_programs(1) + pl.program_id(1)

        @pl.when(step < n_kmean_steps)
        def _():
            _page_block_means(pages, km_ref, pages_per_block)

    kb = k_ref[...]
    vb = v_ref[...]
    kmean = jnp.mean(kb.astype(F32).reshape(n_blocks, blk, HEAD_DIM), axis=1)
    n_rank_rows = -(-n_blocks // 8) * 8
    kmean_pad = jnp.concatenate(
        [kmean, jnp.zeros((HEAD_DIM - n_blocks, HEAD_DIM), F32)], axis=0)
    row = lax.broadcasted_iota(jnp.int32, (blk, blk), 0)
    colv = lax.broadcasted_iota(jnp.int32, (blk, blk), 1)
    causal = colv <= row
    r128 = lax.broadcasted_iota(jnp.int32, (HEAD_DIM, HEAD_DIM), 0)
    c128 = lax.broadcasted_iota(jnp.int32, (HEAD_DIM, HEAD_DIM), 1)
    eye_bf16 = jnp.where(r128 == c128, 1.0, 0.0).astype(BF16)

    for t in range(n_blocks):
        rows = slice(t * blk, (t + 1) * blk)
        qf = q_ref[rows, :]
        s = _dot_nt((qf * scale).astype(BF16), kb[:(t + 1) * blk])
        pieces = []
        if t > MOBA_TOPK:
            gate_t = _dot_nt(kmean_pad, qf, precision=lax.Precision.HIGHEST)
            keep = _block_keep(gate_t[:n_rank_rows], t, eye_bf16)
        for j in range(t):
            sj = s[:, j * blk:(j + 1) * blk]
            if t > MOBA_TOPK:
                sj = jnp.where(keep[:, j:j + 1] > 0.5, sj, NEG_INF)
            pieces.append(sj)
        pieces.append(jnp.where(causal, s[:, t * blk:], NEG_INF))
        s = jnp.concatenate(pieces, axis=1) if len(pieces) > 1 else pieces[0]
        m = jnp.max(s, axis=1, keepdims=True)
        p = jnp.exp(s - m)
        l = jnp.sum(p, axis=1, keepdims=True)
        o = _dot(p.astype(BF16), vb[:(t + 1) * blk]) / l
        o_ref[rows, :] = o.astype(o_ref.dtype)


def _moba_prompt(proj, k2d, v2d, batch, seq, h_att, q_col0, cache, layer, page_table,
                 pages_per_block):
    assert seq % MOBA_BLOCK == 0
    n_blocks = seq // MOBA_BLOCK
    assert n_blocks <= HEAD_DIM
    hd = HEAD_DIM
    db, n_pages = page_table.shape
    page, n_heads = cache.shape[2:4]
    pps = KMEAN_PAGES_PER_STEP
    assert n_pages % pps == 0 and pps % pages_per_block == 0
    blocks_per_step = pps // pages_per_block
    assert blocks_per_step % 8 == 0
    chunks = n_pages // pps
    n_kmean_steps = db * chunks
    assert n_kmean_steps <= batch * h_att, "not enough attention steps to carry the page stream"

    def kmean_pos(b, h):
        s = jnp.minimum(b * h_att + h, n_kmean_steps - 1)
        return s // chunks, s % chunks

    def page_spec(i):
        def index(b, h, pt):
            kb, kc = kmean_pos(b, h)
            return layer, pt[kb, kc * pps + i], 0, 0, 0
        return pl.BlockSpec((1, 1, page, n_heads, hd), index)

    def kmean_index(b, h, pt):
        kb, kc = kmean_pos(b, h)
        return kb, 0, kc, 0

    head = pl.BlockSpec((seq, hd), lambda b, h, pt: (b, h))
    grid_spec = pltpu.PrefetchScalarGridSpec(
        num_scalar_prefetch=1,
        grid=(batch, h_att),
        in_specs=[pl.BlockSpec((seq, hd), lambda b, h, pt: (b, q_col0 + h)), head, head]
        + [page_spec(i) for i in range(pps)],
        out_specs=[head, pl.BlockSpec((1, n_heads, blocks_per_step, hd), kmean_index)],
    )
    return pl.pallas_call(
        functools.partial(_moba_prompt_kernel, n_blocks=n_blocks, pages_per_block=pages_per_block,
                          n_kmean_steps=None if n_kmean_steps == batch * h_att else n_kmean_steps),
        grid_spec=grid_spec,
        out_shape=[jax.ShapeDtypeStruct((batch * seq, h_att * hd), BF16),
                   jax.ShapeDtypeStruct((db, n_heads, n_pages // pages_per_block, hd), F32)],
        compiler_params=_params(2 * pps * page * n_heads * hd * 4 + 24 * MIB, 2),
        name="moba_prompt",
    )(page_table, proj, k2d, v2d, *([cache] * pps))


def _row_to_col(x_row, eye):
    n = x_row.shape[1]
    return jnp.sum(jnp.where(eye, jnp.broadcast_to(x_row, (n, n)), 0.0), axis=1, keepdims=True)


def _ret_sample_kernel(q_ref, k_ref, v_ref, g_ref, s_ref, cos_ref, sin_ref, dec_ref, gn_ref,
                       o_ref, so_ref, *, h_ret):
    half = HEAD_DIM // 2
    scale = HEAD_DIM ** -0.5
    cos = cos_ref[...]
    sin = sin_ref[...]
    q = q_ref[0]
    q = q * cos + pltpu.roll(q, half, 1) * sin
    k = k_ref[0]
    k = (k * cos + pltpu.roll(k, half, 1) * sin) * scale
    v = v_ref[0]
    r0 = lax.broadcasted_iota(jnp.int32, (HEAD_DIM, HEAD_DIM), 0)
    r1 = lax.broadcasted_iota(jnp.int32, (HEAD_DIM, HEAD_DIM), 1)
    eye = r0 == r1
    for h in range(h_ret):
        hh = slice(h, h + 1)
        kcol = _row_to_col(k[hh, :], eye)
        qcol = _row_to_col(q[hh, :], eye)
        s_new = s_ref[0, 0, h] * dec_ref[hh, :] + kcol * v[hh, :]
        so_ref[0, 0, h] = s_new
        o = jnp.sum(qcol * s_new, axis=0, keepdims=True)
        o_ref[0, hh, :] = _group_norm_gate(o, gn_ref[hh, :], g_ref[0, hh, :]).astype(o_ref.dtype)


def _ret_sample(proj3, state, layer, gn_w, pos, dec_batch, h_ret):
    hd = HEAD_DIM
    cos2, sin2 = _rotary_tables(pos)
    dec = jnp.exp(1.0 * _log_decay(h_ret))[:, None] * jnp.ones((1, hd), F32)
    gn = gn_w.astype(F32).reshape(h_ret, hd)
    grp = lambda g: pl.BlockSpec((1, h_ret, hd), lambda b: (b, g, 0))
    full = lambda shape: pl.BlockSpec(shape, lambda b: (0,) * len(shape))
    return pl.pallas_call(
        functools.partial(_ret_sample_kernel, h_ret=h_ret),
        grid=(dec_batch,),
        in_specs=[grp(0), grp(1), grp(2), grp(3),
                  pl.BlockSpec((1, 1, h_ret, hd, hd), lambda b: (layer, b, 0, 0, 0)),
                  full((1, hd)), full((1, hd)), full((h_ret, hd)), full((h_ret, hd))],
        out_specs=[pl.BlockSpec((1, h_ret, hd), lambda b: (b, 0, 0)),
                   pl.BlockSpec((1, 1, h_ret, hd, hd), lambda b: (0, b, 0, 0, 0))],
        out_shape=[jax.ShapeDtypeStruct((dec_batch, h_ret, hd), F32),
                   jax.ShapeDtypeStruct((1, dec_batch, h_ret, hd, hd), F32)],
        compiler_params=_params(16 * MIB, 1),
        name="retention_sample",
    )(proj3, proj3, proj3, proj3, state, cos2, sin2, dec, gn)


def _select_kernel(km_ref, q_ref, o_ref, *, n_sel):
    n_heads, nb = km_ref.shape[1], km_ref.shape[2]
    blk_id = lax.broadcasted_iota(jnp.int32, (nb, 1), 0).astype(F32)
    row = lax.broadcasted_iota(jnp.int32, o_ref.shape[1:], 0)
    lane = lax.broadcasted_iota(jnp.int32, o_ref.shape[1:], 1)
    out = jnp.zeros(o_ref.shape[1:], jnp.int32)
    for h in range(n_heads):
        gate = jnp.sum(km_ref[0, h] * q_ref[0, h:h + 1, :], axis=1, keepdims=True)
        for r in range(n_sel):
            best = jnp.max(gate, axis=0, keepdims=True)
            idx = jnp.min(jnp.where(gate == best, blk_id, float(nb)), axis=0, keepdims=True)
            out = jnp.where((row == r) & (lane == h), idx.astype(jnp.int32), out)
            gate = jnp.where(blk_id == idx, NEG_INF, gate)
    o_ref[0] = out


def _select_sample(kmean, proj3, q_group, n_sel):
    db, n_heads, nb, hd = kmean.shape
    assert n_sel <= 8 and n_heads <= hd
    return pl.pallas_call(
        functools.partial(_select_kernel, n_sel=n_sel),
        grid=(db,),
        in_specs=[pl.BlockSpec((1, n_heads, nb, hd), lambda b: (b, 0, 0, 0)),
                  pl.BlockSpec((1, n_heads, hd), lambda b: (b, q_group, 0))],
        out_specs=pl.BlockSpec((1, 8, hd), lambda b: (b, 0, 0)),
        out_shape=jax.ShapeDtypeStruct((db, 8, hd), jnp.int32),
        compiler_params=_params(16 * MIB, 1),
        name="moba_sample_select",
    )(kmean, proj3)


def _attn_sample_kernel(pt_ref, sel_ref, q_ref, kn_ref, vn_ref, ck_ref, cv_ref, o_ref,
                        kbuf, vbuf, sems, *, layer, n_sel, pages_per_block):
    b = pl.program_id(0)
    n_heads, n_pages_sel, page, hd = kbuf.shape
    scale = hd ** -0.5

    def head_copies(h):
        copies = []
        for r in range(n_sel):
            blk = sel_ref[b * 8 + r, h]
            for i in range(pages_per_block):
                phys = pt_ref[b, blk * pages_per_block + i]
                slot = r * pages_per_block + i
                copies.append(pltpu.make_async_copy(
                    ck_ref.at[layer, phys, :, h, :], kbuf.at[h, slot], sems.at[0, h]))
                copies.append(pltpu.make_async_copy(
                    cv_ref.at[layer, phys, :, h, :], vbuf.at[h, slot], sems.at[1, h]))
        return copies

    for h in range(n_heads):
        for c in head_copies(h):
            c.start()

    for h in range(n_heads):
        for c in head_copies(h):
            c.wait()
        hh = slice(h, h + 1)
        q = q_ref[0, hh, :]
        k_all = kbuf[h].reshape(n_pages_sel * page, hd).astype(BF16)
        v_all = vbuf[h].reshape(n_pages_sel * page, hd).astype(BF16)
        q8 = jnp.broadcast_to(q * scale, (8, hd)).astype(BF16)
        s = _dot_nt(q8, k_all)
        s_own = jnp.sum(kn_ref[0, hh, :] * q, axis=1, keepdims=True) * scale
        m = jnp.maximum(jnp.max(s, axis=1, keepdims=True), s_own)
        e = jnp.exp(s - m)
        e_own = jnp.exp(s_own - m)
        l = jnp.sum(e, axis=1, keepdims=True) + e_own
        o = _dot(e.astype(BF16), v_all) + e_own * vn_ref[0, hh, :]
        o_ref[0, hh, :] = (o / l)[0:1, :]


def _attn_sample(cache_k, cache_v, layer, page_table, sel2, q3, q_group, kn3, vn3,
                 dec_batch, n_sel, pages_per_block):
    page, h_att, hd = cache_k.shape[2:]
    n_pages_sel = n_sel * pages_per_block
    grid_spec = pltpu.PrefetchScalarGridSpec(
        num_scalar_prefetch=2,
        grid=(dec_batch,),
        in_specs=[pl.BlockSpec((1, h_att, hd), lambda b, pt, sel: (b, q_group, 0)),
                  pl.BlockSpec((1, h_att, hd), lambda b, pt, sel: (b, 0, 0)),
                  pl.BlockSpec((1, h_att, hd), lambda b, pt, sel: (b, 0, 0)),
                  pl.BlockSpec(memory_space=pl.ANY),
                  pl.BlockSpec(memory_space=pl.ANY)],
        out_specs=pl.BlockSpec((1, h_att, hd), lambda b, pt, sel: (b, 0, 0)),
        scratch_shapes=[pltpu.VMEM((h_att, n_pages_sel, page, hd), F32),
                        pltpu.VMEM((h_att, n_pages_sel, page, hd), F32),
                        pltpu.SemaphoreType.DMA((2, h_att))],
    )
    return pl.pallas_call(
        functools.partial(_attn_sample_kernel, layer=layer, n_sel=n_sel,
                          pages_per_block=pages_per_block),
        grid_spec=grid_spec,
        out_shape=jax.ShapeDtypeStruct((dec_batch, h_att, hd), F32),
        compiler_params=_params(2 * h_att * n_pages_sel * page * hd * 4 + 16 * MIB, 1),
        name="moba_sample_attn",
    )(page_table, sel2, q3, kn3, vn3, cache_k, cache_v)


def _pad_rows(x, rows):
    return jnp.pad(x, ((0, rows - x.shape[0]), (0, 0)))


def kernel(x_prompt, x_sample, cache_k, cache_v, state_ret, page_table, w_in, w_out, norm_mix,
           ret_gn, norm_ffn, w_gate, w_up, w_down, norm_final):
    batch, seq, d_model = x_prompt.shape
    dec_batch, dec_seq, _ = x_sample.shape
    depth, n_pool, page_size, h_att, hd = cache_k.shape
    h_ret = state_ret.shape[2]
    w_ret, w_att = h_ret * hd, h_att * hd
    n_pages = page_table.shape[1]
    past_len = n_pages * page_size
    assert hd == HEAD_DIM and h_ret == h_att and dec_seq == 1
    assert MOBA_BLOCK % page_size == 0 and past_len % MOBA_BLOCK == 0
    pages_per_block = MOBA_BLOCK // page_size
    n_past_blocks = past_len // MOBA_BLOCK
    assert n_past_blocks >= MOBA_TOPK and dec_batch <= SAMPLE_ROWS
    n_groups = 7
    att_group = 4
    in_cols = w_in.shape[2]
    assert in_cols == n_groups * w_ret

    m_p = batch * seq
    yp = x_prompt.reshape(m_p, d_model)
    ys = _pad_rows(x_sample.reshape(dec_batch, d_model), SAMPLE_ROWS)
    pos_s = past_len + jnp.arange(dec_seq)

    n_mix = 5

    rp_states, ks_rows, vs_rows, rs_states = [], [], [], []
    rows5 = (depth, batch, seq, h_att, hd)
    k_prompt = v_prompt = None
    for l in range(depth):
        hp = _rmsnorm(yp, norm_mix[l], BF16, 512)
        hs = _rmsnorm(ys, norm_mix[l], BF16, SAMPLE_ROWS)
        proj_p, proj_s = _proj(hp, hs, w_in, l, n_mix * w_ret, 1024, 512)
        k_prompt, k2d, k_s = _kv_proj(hp, hs, w_in, l, 5 * w_ret, rows5, k_prompt, 512)
        v_prompt, v2d, v_s = _kv_proj(hp, hs, w_in, l, 6 * w_ret, rows5, v_prompt, 512)

        ret_o, ret_s, w_down_b = _ret_prompt(proj_p, ret_gn[l], batch, seq, h_ret, w_down, l)
        att_o, kmean = _moba_prompt(proj_p, k2d, v2d, batch, seq, h_att, att_group * h_ret,
                                    cache_k, l, page_table, pages_per_block)
        rp_states.append(ret_s)

        proj3 = proj_s.reshape(SAMPLE_ROWS, n_mix * h_ret, hd)
        kn3 = k_s.reshape(SAMPLE_ROWS, h_att, hd)
        vn3 = v_s.reshape(SAMPLE_ROWS, h_att, hd)
        ret_os, ret_ss = _ret_sample(proj3, state_ret, l, ret_gn[l], pos_s, dec_batch, h_ret)
        sel = _select_sample(kmean, proj3, att_group, MOBA_TOPK)
        att_os = _attn_sample(cache_k, cache_v, l, page_table, sel.reshape(dec_batch * 8, hd),
                              proj3, att_group, kn3, vn3, dec_batch, MOBA_TOPK, pages_per_block)
        ks_rows.append(k_s[:dec_batch].reshape(dec_batch, 1, h_att, hd))
        vs_rows.append(v_s[:dec_batch].reshape(dec_batch, 1, h_att, hd))
        rs_states.append(ret_ss[0])
        mix_r = _pad_rows(ret_os.reshape(dec_batch, w_ret), SAMPLE_ROWS).astype(BF16)
        mix_a = _pad_rows(att_os.reshape(dec_batch, w_att), SAMPLE_ROWS).astype(BF16)

        yp, ys = _outproj(ret_o, att_o, mix_r, mix_a, w_out, l, yp, ys, 1024, 512)
        hf = _rmsnorm(yp, norm_ffn[l], BF16, 512)
        hfs = _rmsnorm(ys, norm_ffn[l], BF16, SAMPLE_ROWS)
        act, acts = _gateup(hf, hfs, w_gate, w_up, l, 1024, 256)
        yp, ys = _down(act, acts, w_down_b, 0, yp, ys, 512, 512)

    y_prompt = _rmsnorm(yp, norm_final, F32, 512).reshape(batch, seq, d_model)
    y_sample = _rmsnorm(ys, norm_final, F32, SAMPLE_ROWS)[:dec_batch].reshape(dec_batch, 1, d_model)
    return (y_prompt, y_sample, k_prompt, v_prompt, jnp.stack(rp_states),
            jnp.stack(ks_rows), jnp.stack(vs_rows), jnp.stack(rs_states))
```

```python
import functools
import math

import jax
import jax.numpy as jnp
from jax import lax
from jax.experimental import pallas as pl
from jax.experimental.pallas import tpu as pltpu

HEAD_DIM = 128
RET_CHUNK = 128
MOBA_BLOCK = 256
MOBA_TOPK = 3
ROPE_BASE = 10000.0
NORM_EPS = 1e-6
GN_EPS = 1e-5

BF16 = jnp.bfloat16
F32 = jnp.float32
NEG_INF = float("-inf")
MIB = 1024 * 1024
V7X_VMEM_BUDGET = 56 * MIB
SAMPLE_ROWS = 16
KMEAN_PAGES_PER_STEP = 16
MOBA_QROWS = 256


def _params(vmem_bytes, n_axes):
    return pltpu.CompilerParams(
        dimension_semantics=("arbitrary",) * n_axes,
        vmem_limit_bytes=int(min(max(vmem_bytes, 16 * MIB), V7X_VMEM_BUDGET)))


def _dot(a, b):
    return jnp.dot(a, b, preferred_element_type=F32)


def _dot_nt(a, b, precision=None):
    return lax.dot_general(a, b, (((1,), (1,)), ((), ())), precision=precision,
                           preferred_element_type=F32)


def _dot_tn(a, b):
    return lax.dot_general(a, b, (((0,), (0,)), ((), ())), preferred_element_type=F32)


def _silu(x):
    return x * (1.0 / (1.0 + jnp.exp(-x)))


def _rmsnorm_kernel(x_ref, g_ref, o_ref):
    x = x_ref[...]
    y = x * lax.rsqrt(jnp.mean(x * x, axis=-1, keepdims=True) + NORM_EPS)
    o_ref[...] = (y * g_ref[...]).astype(o_ref.dtype)


def _rmsnorm(x, g, out_dtype, tm):
    m, d = x.shape
    return pl.pallas_call(
        _rmsnorm_kernel,
        grid=(m // tm,),
        in_specs=[pl.BlockSpec((tm, d), lambda i: (i, 0)),
                  pl.BlockSpec((1, d), lambda i: (0, 0))],
        out_specs=pl.BlockSpec((tm, d), lambda i: (i, 0)),
        out_shape=jax.ShapeDtypeStruct((m, d), out_dtype),
        compiler_params=_params(6 * tm * d * 4, 1),
        name="rmsnorm",
    )(x, g.reshape(1, d))


def _first_m_step():
    return pl.program_id(1) == 0


def _m_tile(n, m, m_tiles):
    return jnp.where(n % 2 == 0, m, m_tiles - 1 - m)


def _a_spec(tm, width, m_tiles):
    return pl.BlockSpec((tm, width), lambda n, m: (_m_tile(n, m, m_tiles), 0))


def _o_spec(tm, tn, m_tiles):
    return pl.BlockSpec((tm, tn), lambda n, m: (_m_tile(n, m, m_tiles), n))


def _weight_spec(k, tn, layer, col_block0=0, single_buffer=False):
    mode = pl.Buffered(1) if single_buffer else None
    return pl.BlockSpec((None, k, tn), lambda n, m: (layer, 0, col_block0 + n), pipeline_mode=mode)


def _proj_kernel(a_ref, as_ref, w_ref, o_ref, os_ref, wb_ref):
    @pl.when(_first_m_step())
    def _():
        wb_ref[...] = w_ref[...].astype(BF16)
        os_ref[...] = _dot(as_ref[...], wb_ref[...])

    o_ref[...] = _dot(a_ref[...], wb_ref[...])


def _proj(a, a_s, w, layer, n_cols, tm, tn):
    m, k = a.shape
    rows_s = a_s.shape[0]
    assert m % tm == 0 and n_cols % tn == 0
    return pl.pallas_call(
        _proj_kernel,
        grid=(n_cols // tn, m // tm),
        in_specs=[_a_spec(tm, k, m // tm),
                  pl.BlockSpec((rows_s, k), lambda n, m: (0, 0)),
                  _weight_spec(k, tn, layer)],
        out_specs=[_o_spec(tm, tn, m // tm),
                   pl.BlockSpec((rows_s, tn), lambda n, m: (0, n))],
        out_shape=[jax.ShapeDtypeStruct((m, n_cols), F32),
                   jax.ShapeDtypeStruct((rows_s, n_cols), F32)],
        scratch_shapes=[pltpu.VMEM((k, tn), BF16)],
        compiler_params=_params(2 * (tm * k * 2 + k * tn * 4 + tm * tn * 4) + k * tn * 2 + 8 * MIB, 2),
        name="in_proj",
    )(a, a_s, w)


def _kv_proj_kernel(*refs, layer, heads_per_step, tiles_per_seq, n_tiles, m_tiles):
    a_ref, as_ref, w_ref = refs[:3]
    o5_ref, o2_ref, os_ref, wb_ref, stage_ref, sems = refs[-6:]
    tm = a_ref.shape[0]
    step = pl.program_id(0) * m_tiles + pl.program_id(1)

    def row_copies(at_step):
        n, m = at_step // m_tiles, at_step % m_tiles
        tile = _m_tile(n, m, m_tiles)
        b, row0 = tile // tiles_per_seq, (tile % tiles_per_seq) * tm
        slot = at_step % 2
        return [pltpu.make_async_copy(
            stage_ref.at[slot, :, pl.ds(h * HEAD_DIM, HEAD_DIM)],
            o5_ref.at[layer, b, pl.ds(row0, tm), n * heads_per_step + h, :],
            sems.at[slot]) for h in range(heads_per_step)]

    @pl.when(_first_m_step())
    def _():
        wb_ref[...] = w_ref[...].astype(BF16)
        os_ref[...] = _dot(as_ref[...], wb_ref[...])

    acc = _dot(a_ref[...], wb_ref[...])
    o2_ref[...] = acc.astype(o2_ref.dtype)

    @pl.when(step >= 2)
    def _():
        for c in row_copies(step - 2):
            c.wait()

    stage_ref[step % 2] = acc
    for c in row_copies(step):
        c.start()

    @pl.when(step == n_tiles * m_tiles - 1)
    def _():
        if n_tiles * m_tiles >= 2:
            for c in row_copies(step - 1):
                c.wait()
        for c in row_copies(step):
            c.wait()


def _kv_proj(a, a_s, w, layer, col0, rows5, prev, tm):
    m, k = a.shape
    rows_s = a_s.shape[0]
    depth, batch, seq, n_heads, hd = rows5
    heads_per_step = 8
    tn = heads_per_step * hd
    assert seq % tm == 0 and n_heads % heads_per_step == 0 and col0 % tn == 0
    tiles_per_seq = seq // tm
    m_tiles = m // tm
    in_specs = [_a_spec(tm, k, m // tm),
                pl.BlockSpec((rows_s, k), lambda n, m: (0, 0)),
                _weight_spec(k, tn, layer, col0 // tn, single_buffer=True)]
    args = [a, a_s, w]
    aliases = {}
    if prev is not None:
        in_specs.append(pl.BlockSpec(memory_space=pl.ANY))
        args.append(prev)
        aliases = {3: 0}
    n_tiles = n_heads // heads_per_step
    return pl.pallas_call(
        functools.partial(_kv_proj_kernel, layer=layer, heads_per_step=heads_per_step,
                          tiles_per_seq=tiles_per_seq, n_tiles=n_tiles, m_tiles=m_tiles),
        grid=(n_tiles, m_tiles),
        in_specs=in_specs,
        out_specs=[pl.BlockSpec(memory_space=pl.ANY),
                   _o_spec(tm, tn, m_tiles),
                   pl.BlockSpec((rows_s, tn), lambda n, m: (0, n))],
        out_shape=[jax.ShapeDtypeStruct(rows5, F32),
                   jax.ShapeDtypeStruct((m, n_heads * hd), BF16),
                   jax.ShapeDtypeStruct((rows_s, n_heads * hd), F32)],
        scratch_shapes=[pltpu.VMEM((k, tn), BF16), pltpu.VMEM((2, tm, tn), F32),
                        pltpu.SemaphoreType.DMA((2,))],
        input_output_aliases=aliases,
        compiler_params=_params(
            2 * (tm * k * 2 + tm * tn * 6) + k * tn * 6 + 8 * MIB, 2),
        name="kv_proj",
    )(*args)


def _outproj_kernel(a1_ref, a2_ref, a1s_ref, a2s_ref, w_ref, r_ref, rs_ref, o_ref, os_ref, wb_ref):
    k1 = a1_ref.shape[1]

    def mix(x1_ref, x2_ref, res_ref):
        return res_ref[...] + (_dot(x1_ref[...], wb_ref[:k1, :]) + _dot(x2_ref[...], wb_ref[k1:, :]))

    @pl.when(_first_m_step())
    def _():
        wb_ref[...] = w_ref[...].astype(BF16)
        os_ref[...] = mix(a1s_ref, a2s_ref, rs_ref)

    o_ref[...] = mix(a1_ref, a2_ref, r_ref)


def _outproj(a1, a2, a1_s, a2_s, w, layer, res, res_s, tm, tn):
    m, k1 = a1.shape
    k2 = a2.shape[1]
    rows_s = a1_s.shape[0]
    k, n = k1 + k2, w.shape[2]
    assert m % tm == 0 and n % tn == 0
    return pl.pallas_call(
        _outproj_kernel,
        grid=(n // tn, m // tm),
        in_specs=[_a_spec(tm, k1, m // tm),
                  _a_spec(tm, k2, m // tm),
                  pl.BlockSpec((rows_s, k1), lambda n, m: (0, 0)),
                  pl.BlockSpec((rows_s, k2), lambda n, m: (0, 0)),
                  _weight_spec(k, tn, layer),
                  _o_spec(tm, tn, m // tm),
                  pl.BlockSpec((rows_s, tn), lambda n, m: (0, n))],
        out_specs=[_o_spec(tm, tn, m // tm),
                   pl.BlockSpec((rows_s, tn), lambda n, m: (0, n))],
        out_shape=[jax.ShapeDtypeStruct((m, n), F32),
                   jax.ShapeDtypeStruct((rows_s, n), F32)],
        scratch_shapes=[pltpu.VMEM((k, tn), BF16)],
        compiler_params=_params(
            2 * (tm * k * 2 + k * tn * 4 + 2 * tm * tn * 4) + k * tn * 2 + 8 * MIB, 2),
        name="out_proj",
    )(a1, a2, a1_s, a2_s, w, res, res_s)


def _gateup_kernel(h_ref, hs_ref, wg_ref, wu_ref, o_ref, os_ref, wgb_ref, wub_ref):
    def swiglu_in(x_ref):
        x = x_ref[...]
        return _silu(_dot(x, wgb_ref[...])) * _dot(x, wub_ref[...])

    @pl.when(_first_m_step())
    def _():
        wgb_ref[...] = wg_ref[...].astype(BF16)
        wub_ref[...] = wu_ref[...].astype(BF16)
        os_ref[...] = swiglu_in(hs_ref).astype(os_ref.dtype)

    o_ref[...] = swiglu_in(h_ref).astype(o_ref.dtype)


def _gateup(h, h_s, wg, wu, layer, tm, tn):
    m, k = h.shape
    rows_s = h_s.shape[0]
    n = wg.shape[2]
    assert m % tm == 0 and n % tn == 0
    return pl.pallas_call(
        _gateup_kernel,
        grid=(n // tn, m // tm),
        in_specs=[_a_spec(tm, k, m // tm),
                  pl.BlockSpec((rows_s, k), lambda n, m: (0, 0)),
                  _weight_spec(k, tn, layer), _weight_spec(k, tn, layer)],
        out_specs=[_o_spec(tm, tn, m // tm),
                   pl.BlockSpec((rows_s, tn), lambda n, m: (0, n))],
        out_shape=[jax.ShapeDtypeStruct((m, n), BF16),
                   jax.ShapeDtypeStruct((rows_s, n), BF16)],
        scratch_shapes=[pltpu.VMEM((k, tn), BF16), pltpu.VMEM((k, tn), BF16)],
        compiler_params=_params(
            2 * (tm * k * 2 + 2 * k * tn * 4 + tm * tn * 2) + 2 * k * tn * 2 + 8 * MIB, 2),
        name="ffn_gate_up",
    )(h, h_s, wg, wu)


def _down_kernel(a_ref, as_ref, w_ref, r_ref, rs_ref, o_ref, os_ref):
    @pl.when(_first_m_step())
    def _():
        os_ref[...] = rs_ref[...] + _dot(as_ref[...], w_ref[...])

    o_ref[...] = r_ref[...] + _dot(a_ref[...], w_ref[...])


def _down(a, a_s, w_bf16, layer, res, res_s, tm, tn):
    m, k = a.shape
    rows_s = a_s.shape[0]
    n = w_bf16.shape[2]
    assert m % tm == 0 and n % tn == 0
    return pl.pallas_call(
        _down_kernel,
        grid=(n // tn, m // tm),
        in_specs=[_a_spec(tm, k, m // tm),
                  pl.BlockSpec((rows_s, k), lambda n, m: (0, 0)),
                  _weight_spec(k, tn, layer),
                  _o_spec(tm, tn, m // tm),
                  pl.BlockSpec((rows_s, tn), lambda n, m: (0, n))],
        out_specs=[_o_spec(tm, tn, m // tm),
                   pl.BlockSpec((rows_s, tn), lambda n, m: (0, n))],
        out_shape=[jax.ShapeDtypeStruct((m, n), F32),
                   jax.ShapeDtypeStruct((rows_s, n), F32)],
        compiler_params=_params(2 * (tm * k * 2 + k * tn * 2 + 2 * tm * tn * 4) + 8 * MIB, 2),
        name="ffn_down",
    )(a, a_s, w_bf16, res, res_s)


def _rotary_tables(pos):
    half = HEAD_DIM // 2
    inv = ROPE_BASE ** (-jnp.arange(half, dtype=F32) / half)
    ang = pos.astype(F32)[:, None] * inv[None, :]
    cos, sin = jnp.cos(ang), jnp.sin(ang)
    return (jnp.concatenate([cos, cos], axis=-1), jnp.concatenate([-sin, sin], axis=-1))


def _log_decay(h_ret):
    return jnp.log1p(-jnp.exp2(-5.0 - jnp.arange(h_ret, dtype=F32)))


def _group_norm_gate(o, gn, g):
    mu = jnp.mean(o, axis=-1, keepdims=True)
    d = o - mu
    var = jnp.mean(d * d, axis=-1, keepdims=True)
    return _silu(g) * (d * lax.rsqrt(var + GN_EPS) * gn)


def _ret_prompt_kernel(q_ref, k_ref, v_ref, g_ref, cos_ref, sin_ref, intra_ref, qdec_ref,
                       kdec_ref, cdec_ref, gn_ref, w_ref, o_ref, s_ref, wb_ref,
                       qb_ref, kb_ref, qd_ref, kd_ref, vb_ref, of_ref,
                       *, chunk, n_chunks, unroll, n_cast_steps):
    half = HEAD_DIM // 2
    scale = HEAD_DIM ** -0.5
    seq = q_ref.shape[0]
    intra = intra_ref[0]
    cdec = cdec_ref[0, 0:1, :]

    step = pl.program_id(0) * pl.num_programs(1) + pl.program_id(1)

    @pl.when(step < n_cast_steps)
    def _():
        wb_ref[...] = w_ref[...].astype(BF16)

    cos = cos_ref[...]
    sin = sin_ref[...]
    q = q_ref[...]
    q = q * cos + pltpu.roll(q, half, 1) * sin
    k = k_ref[...]
    k = (k * cos + pltpu.roll(k, half, 1) * sin) * scale
    qb_ref[...] = q.astype(BF16)
    kb_ref[...] = k.astype(BF16)
    qd = q.reshape(n_chunks, chunk, HEAD_DIM) * qdec_ref[...]
    kd = k.reshape(n_chunks, chunk, HEAD_DIM) * kdec_ref[...]
    qd_ref[...] = qd.reshape(seq, HEAD_DIM).astype(BF16)
    kd_ref[...] = kd.reshape(seq, HEAD_DIM).astype(BF16)
    vb_ref[...] = v_ref[...].astype(BF16)

    def body(c, s):
        r = pl.ds(pl.multiple_of(c * chunk, chunk), chunk)
        v = vb_ref[r, :]
        sc = _dot_nt(qb_ref[r, :], kb_ref[r, :]) * intra
        of_ref[r, :] = _dot(sc.astype(BF16), v) + _dot(qd_ref[r, :], s.astype(BF16))
        return s * cdec + _dot_tn(kd_ref[r, :], v)

    s_fin = lax.fori_loop(0, n_chunks, body, jnp.zeros((HEAD_DIM, HEAD_DIM), F32),
                          unroll=unroll)
    s_ref[0, 0] = s_fin
    o_ref[...] = _group_norm_gate(of_ref[...], gn_ref[0], g_ref[...]).astype(o_ref.dtype)


def _slab_rows(k, steps):
    for rows in range(16, k + 1, 16):
        if k % rows == 0 and k // rows <= steps:
            return rows
    raise ValueError(f"cannot split {k} rows into at most {steps} slabs")


def _ret_prompt(proj, gn_w, batch, seq, h_ret, w_stack, layer):
    w_k, w_n = w_stack.shape[1:]
    w_rows = _slab_rows(w_k, batch * h_ret)
    n_cast_steps = w_k // w_rows

    def w_index(b, h):
        return jnp.minimum(b * h_ret + h, n_cast_steps - 1), 0

    chunk = math.gcd(seq, RET_CHUNK)
    n_chunks = seq // chunk
    cos2, sin2 = _rotary_tables(jnp.arange(seq))
    lg = _log_decay(h_ret)
    i = jnp.arange(chunk, dtype=F32)
    rel = i[:, None] - i[None, :]
    intra = jnp.exp(jnp.where(rel[None] >= 0, rel[None] * lg[:, None, None], -jnp.inf))
    ones = jnp.ones((1, 1, HEAD_DIM), F32)
    qdec = jnp.exp((i + 1.0)[None, :, None] * lg[:, None, None]) * ones
    kdec = jnp.exp((chunk - 1.0 - i)[None, :, None] * lg[:, None, None]) * ones
    cdec = jnp.exp(chunk * lg)[:, None, None] * jnp.ones((1, 8, HEAD_DIM), F32)
    gn = gn_w.astype(F32).reshape(h_ret, 1, HEAD_DIM)
    hd = HEAD_DIM

    def col(group):
        return pl.BlockSpec((seq, hd), lambda b, h: (b, group * h_ret + h))

    const = lambda shape: pl.BlockSpec(shape, lambda b, h: (h, 0, 0))
    unroll = math.gcd(n_chunks, 16)
    return pl.pallas_call(
        functools.partial(_ret_prompt_kernel, chunk=chunk, n_chunks=n_chunks, unroll=unroll,
                          n_cast_steps=n_cast_steps),
        grid=(batch, h_ret),
        in_specs=[col(0), col(1), col(2), col(3),
                  pl.BlockSpec((seq, hd), lambda b, h: (0, 0)),
                  pl.BlockSpec((seq, hd), lambda b, h: (0, 0)),
                  const((1, chunk, chunk)), const((1, chunk, hd)), const((1, chunk, hd)),
                  const((1, 8, hd)), const((1, 1, hd)),
                  pl.BlockSpec((None, w_rows, w_n), lambda b, h: (layer,) + w_index(b, h))],
        out_specs=[pl.BlockSpec((seq, hd), lambda b, h: (b, h)),
                   pl.BlockSpec((1, 1, hd, hd), lambda b, h: (b, h, 0, 0)),
                   pl.BlockSpec((None, w_rows, w_n), lambda b, h: (0,) + w_index(b, h))],
        out_shape=[jax.ShapeDtypeStruct((batch * seq, h_ret * hd), BF16),
                   jax.ShapeDtypeStruct((batch, h_ret, hd, hd), F32),
                   jax.ShapeDtypeStruct((1, w_k, w_n), BF16)],
        scratch_shapes=[pltpu.VMEM((seq, hd), BF16)] * 5 + [pltpu.VMEM((seq, hd), F32)],
        compiler_params=_params(2 * 7 * seq * hd * 4 + 2 * w_rows * w_n * 6 + 16 * MIB, 2),
        name="retention_prompt",
    )(proj, proj, proj, proj, cos2, sin2, intra, qdec, kdec, cdec, gn, w_stack)


def _block_keep(gate_t, n_past, eye_bf16):
    blk_id = lax.broadcasted_iota(jnp.int32, gate_t.shape, 0)
    g = jnp.where(blk_id < n_past, gate_t, NEG_INF)
    rank = jnp.zeros(gate_t.shape, F32)
    for i in range(n_past):
        gi = g[i:i + 1, :]
        rank = rank + jnp.where(gi > g, 1.0,
                                jnp.where(gi == g, jnp.where(blk_id > i, 1.0, 0.0), 0.0))
    keep_t = jnp.where(blk_id < n_past, jnp.where(rank < float(MOBA_TOPK), 1.0, 0.0), 0.0)
    pad = jnp.zeros((HEAD_DIM - gate_t.shape[0], gate_t.shape[1]), F32)
    keep_t = jnp.concatenate([keep_t, pad], axis=0).astype(BF16)
    return _dot_tn(keep_t, eye_bf16)


def _page_block_means(pages, o_ref, pages_per_block):
    n_out = len(pages) // pages_per_block
    n_heads = pages[0].shape[3]
    inv_n = 1.0 / (pages[0].shape[2] * pages_per_block)
    for i in range(n_out):
        acc = jnp.sum(pages[i * pages_per_block][0, 0], axis=0)
        for p in range(1, pages_per_block):
            acc = acc + jnp.sum(pages[i * pages_per_block + p][0, 0], axis=0)
        acc = acc * inv_n
        for h in range(n_heads):
            o_ref[0, h, i:i + 1, :] = acc[h:h + 1, :]


def _moba_prompt_kernel(pt_ref, q_ref, k_ref, v_ref, *refs, n_blocks, pages_per_block,
                        n_kmean_steps):
    del pt_ref
    pages, (o_ref, km_ref) = refs[:-2], refs[-2:]
    blk = MOBA_BLOCK
    scale = HEAD_DIM ** -0.5

    if n_kmean_steps is None:
        _page_block_means(pages, km_ref, pages_per_block)
    else:
        step = pl.program_id(0) * pl.num_programs(1) + pl.program_id(1)

        @pl.when(step < n_kmean_steps)
        def _():
            _page_block_means(pages, km_ref, pages_per_block)

    kb = k_ref[...]
    vb = v_ref[...]
    kmean = jnp.mean(kb.astype(F32).reshape(n_blocks, blk, HEAD_DIM), axis=1)
    n_rank_rows = -(-n_blocks // 8) * 8
    kmean_pad = jnp.concatenate(
        [kmean, jnp.zeros((HEAD_DIM - n_blocks, HEAD_DIM), F32)], axis=0)
    qrows = MOBA_QROWS
    row = lax.broadcasted_iota(jnp.int32, (qrows, blk), 0)
    colv = lax.broadcasted_iota(jnp.int32, (qrows, blk), 1)
    r128 = lax.broadcasted_iota(jnp.int32, (HEAD_DIM, HEAD_DIM), 0)
    c128 = lax.broadcasted_iota(jnp.int32, (HEAD_DIM, HEAD_DIM), 1)
    eye_bf16 = jnp.where(r128 == c128, 1.0, 0.0).astype(BF16)

    def start_tile(t, u):
        r0 = t * blk + u * qrows
        qf = q_ref[r0:r0 + qrows, :]
        keep = None
        if t > MOBA_TOPK:
            gate_t = _dot_nt(kmean_pad, qf, precision=lax.Precision.HIGHEST)
            keep = _block_keep(gate_t[:n_rank_rows], t, eye_bf16)
        return dict(t=t, r0=r0, causal=colv <= row + u * qrows, qs=(qf * scale).astype(BF16),
                    keep=keep, m=None, l=None, acc=None)

    def attend(tile, j):
        t = tile["t"]
        kv = slice(j * blk, (j + 1) * blk)
        s = _dot_nt(tile["qs"], kb[kv])
        if j == t:
            s = jnp.where(tile["causal"], s, NEG_INF)
        elif tile["keep"] is not None:
            s = jnp.where(tile["keep"][:, j:j + 1] > 0.5, s, NEG_INF)
        s_max = jnp.max(s, axis=1, keepdims=True)
        if tile["m"] is None:
            m = s_max
            p = jnp.exp(s - m)
            tile["l"] = jnp.sum(p, axis=1, keepdims=True)
            tile["acc"] = _dot(p.astype(BF16), vb[kv])
        else:
            m = jnp.maximum(tile["m"], s_max)
            alpha = jnp.exp(tile["m"] - m)
            p = jnp.exp(s - m)
            tile["l"] = alpha * tile["l"] + jnp.sum(p, axis=1, keepdims=True)
            tile["acc"] = alpha * tile["acc"] + _dot(p.astype(BF16), vb[kv])
        tile["m"] = m

    for t in range(n_blocks):
        for u in range(blk // qrows):
            tile = start_tile(t, u)
            for j in [t] + list(range(t)):
                attend(tile, j)
            o_ref[tile["r0"]:tile["r0"] + qrows, :] = (tile["acc"] / tile["l"]).astype(o_ref.dtype)


def _moba_prompt(proj, k2d, v2d, batch, seq, h_att, q_col0, cache, layer, page_table,
                 pages_per_block):
    assert seq % MOBA_BLOCK == 0
    n_blocks = seq // MOBA_BLOCK
    assert n_blocks <= HEAD_DIM
    hd = HEAD_DIM
    db, n_pages = page_table.shape
    page, n_heads = cache.shape[2:4]
    pps = KMEAN_PAGES_PER_STEP
    assert n_pages % pps == 0 and pps % pages_per_block == 0
    blocks_per_step = pps // pages_per_block
    assert blocks_per_step % 8 == 0
    chunks = n_pages // pps
    n_kmean_steps = db * chunks
    assert n_kmean_steps <= batch * h_att, "not enough attention steps to carry the page stream"

    def kmean_pos(b, h):
        s = jnp.minimum(b * h_att + h, n_kmean_steps - 1)
        return s // chunks, s % chunks

    def page_spec(i):
        def index(b, h, pt):
            kb, kc = kmean_pos(b, h)
            return layer, pt[kb, kc * pps + i], 0, 0, 0
        return pl.BlockSpec((1, 1, page, n_heads, hd), index)

    def kmean_index(b, h, pt):
        kb, kc = kmean_pos(b, h)
        return kb, 0, kc, 0

    head = pl.BlockSpec((seq, hd), lambda b, h, pt: (b, h))
    grid_spec = pltpu.PrefetchScalarGridSpec(
        num_scalar_prefetch=1,
        grid=(batch, h_att),
        in_specs=[pl.BlockSpec((seq, hd), lambda b, h, pt: (b, q_col0 + h)), head, head]
        + [page_spec(i) for i in range(pps)],
        out_specs=[head, pl.BlockSpec((1, n_heads, blocks_per_step, hd), kmean_index)],
    )
    return pl.pallas_call(
        functools.partial(_moba_prompt_kernel, n_blocks=n_blocks, pages_per_block=pages_per_block,
                          n_kmean_steps=None if n_kmean_steps == batch * h_att else n_kmean_steps),
        grid_spec=grid_spec,
        out_shape=[jax.ShapeDtypeStruct((batch * seq, h_att * hd), BF16),
                   jax.ShapeDtypeStruct((db, n_heads, n_pages // pages_per_block, hd), F32)],
        compiler_params=_params(2 * pps * page * n_heads * hd * 4 + 24 * MIB, 2),
        name="moba_prompt",
    )(page_table, proj, k2d, v2d, *([cache] * pps))


def _row_to_col(x_row, eye):
    n = x_row.shape[1]
    return jnp.sum(jnp.where(eye, jnp.broadcast_to(x_row, (n, n)), 0.0), axis=1, keepdims=True)


def _ret_sample_kernel(q_ref, k_ref, v_ref, g_ref, s_ref, cos_ref, sin_ref, dec_ref, gn_ref,
                       o_ref, so_ref, *, h_ret):
    half = HEAD_DIM // 2
    scale = HEAD_DIM ** -0.5
    cos = cos_ref[...]
    sin = sin_ref[...]
    q = q_ref[0]
    q = q * cos + pltpu.roll(q, half, 1) * sin
    k = k_ref[0]
    k = (k * cos + pltpu.roll(k, half, 1) * sin) * scale
    v = v_ref[0]
    r0 = lax.broadcasted_iota(jnp.int32, (HEAD_DIM, HEAD_DIM), 0)
    r1 = lax.broadcasted_iota(jnp.int32, (HEAD_DIM, HEAD_DIM), 1)
    eye = r0 == r1
    for h in range(h_ret):
        hh = slice(h, h + 1)
        kcol = _row_to_col(k[hh, :], eye)
        qcol = _row_to_col(q[hh, :], eye)
        s_new = s_ref[0, 0, h] * dec_ref[hh, :] + kcol * v[hh, :]
        so_ref[0, 0, h] = s_new
        o = jnp.sum(qcol * s_new, axis=0, keepdims=True)
        o_ref[0, hh, :] = _group_norm_gate(o, gn_ref[hh, :], g_ref[0, hh, :]).astype(o_ref.dtype)


def _ret_sample(proj3, state, layer, gn_w, pos, dec_batch, h_ret):
    hd = HEAD_DIM
    cos2, sin2 = _rotary_tables(pos)
    dec = jnp.exp(1.0 * _log_decay(h_ret))[:, None] * jnp.ones((1, hd), F32)
    gn = gn_w.astype(F32).reshape(h_ret, hd)
    grp = lambda g: pl.BlockSpec((1, h_ret, hd), lambda b: (b, g, 0))
    full = lambda shape: pl.BlockSpec(shape, lambda b: (0,) * len(shape))
    return pl.pallas_call(
        functools.partial(_ret_sample_kernel, h_ret=h_ret),
        grid=(dec_batch,),
        in_specs=[grp(0), grp(1), grp(2), grp(3),
                  pl.BlockSpec((1, 1, h_ret, hd, hd), lambda b: (layer, b, 0, 0, 0)),
                  full((1, hd)), full((1, hd)), full((h_ret, hd)), full((h_ret, hd))],
        out_specs=[pl.BlockSpec((1, h_ret, hd), lambda b: (b, 0, 0)),
                   pl.BlockSpec((1, 1, h_ret, hd, hd), lambda b: (0, b, 0, 0, 0))],
        out_shape=[jax.ShapeDtypeStruct((dec_batch, h_ret, hd), F32),
                   jax.ShapeDtypeStruct((1, dec_batch, h_ret, hd, hd), F32)],
        compiler_params=_params(16 * MIB, 1),
        name="retention_sample",
    )(proj3, proj3, proj3, proj3, state, cos2, sin2, dec, gn)


def _select_kernel(km_ref, q_ref, o_ref, *, n_sel):
    n_heads, nb = km_ref.shape[1], km_ref.shape[2]
    blk_id = lax.broadcasted_iota(jnp.int32, (nb, 1), 0).astype(F32)
    row = lax.broadcasted_iota(jnp.int32, o_ref.shape[1:], 0)
    lane = lax.broadcasted_iota(jnp.int32, o_ref.shape[1:], 1)
    out = jnp.zeros(o_ref.shape[1:], jnp.int32)
    for h in range(n_heads):
        gate = jnp.sum(km_ref[0, h] * q_ref[0, h:h + 1, :], axis=1, keepdims=True)
        for r in range(n_sel):
            best = jnp.max(gate, axis=0, keepdims=True)
            idx = jnp.min(jnp.where(gate == best, blk_id, float(nb)), axis=0, keepdims=True)
            out = jnp.where((row == r) & (lane == h), idx.astype(jnp.int32), out)
            gate = jnp.where(blk_id == idx, NEG_INF, gate)
    o_ref[0] = out


def _select_sample(kmean, proj3, q_group, n_sel):
    db, n_heads, nb, hd = kmean.shape
    assert n_sel <= 8 and n_heads <= hd
    return pl.pallas_call(
        functools.partial(_select_kernel, n_sel=n_sel),
        grid=(db,),
        in_specs=[pl.BlockSpec((1, n_heads, nb, hd), lambda b: (b, 0, 0, 0)),
                  pl.BlockSpec((1, n_heads, hd), lambda b: (b, q_group, 0))],
        out_specs=pl.BlockSpec((1, 8, hd), lambda b: (b, 0, 0)),
        out_shape=jax.ShapeDtypeStruct((db, 8, hd), jnp.int32),
        compiler_params=_params(16 * MIB, 1),
        name="moba_sample_select",
    )(kmean, proj3)


def _attn_sample_kernel(pt_ref, sel_ref, q_ref, kn_ref, vn_ref, ck_ref, cv_ref, o_ref,
                        kbuf, vbuf, sems, *, layer, n_sel, pages_per_block):
    b = pl.program_id(0)
    n_heads, n_pages_sel, page, hd = kbuf.shape
    scale = hd ** -0.5

    def head_copies(h):
        copies = []
        for r in range(n_sel):
            blk = sel_ref[b * 8 + r, h]
            for i in range(pages_per_block):
                phys = pt_ref[b, blk * pages_per_block + i]
                slot = r * pages_per_block + i
                copies.append(pltpu.make_async_copy(
                    ck_ref.at[layer, phys, :, h, :], kbuf.at[h, slot], sems.at[0, h]))
                copies.append(pltpu.make_async_copy(
                    cv_ref.at[layer, phys, :, h, :], vbuf.at[h, slot], sems.at[1, h]))
        return copies

    for h in range(n_heads):
        for c in head_copies(h):
            c.start()

    for h in range(n_heads):
        for c in head_copies(h):
            c.wait()
        hh = slice(h, h + 1)
        q = q_ref[0, hh, :]
        k_all = kbuf[h].reshape(n_pages_sel * page, hd).astype(BF16)
        v_all = vbuf[h].reshape(n_pages_sel * page, hd).astype(BF16)
        q8 = jnp.broadcast_to(q * scale, (8, hd)).astype(BF16)
        s = _dot_nt(q8, k_all)
        s_own = jnp.sum(kn_ref[0, hh, :] * q, axis=1, keepdims=True) * scale
        m = jnp.maximum(jnp.max(s, axis=1, keepdims=True), s_own)
        e = jnp.exp(s - m)
        e_own = jnp.exp(s_own - m)
        l = jnp.sum(e, axis=1, keepdims=True) + e_own
        o = _dot(e.astype(BF16), v_all) + e_own * vn_ref[0, hh, :]
        o_ref[0, hh, :] = (o / l)[0:1, :]


def _attn_sample(cache_k, cache_v, layer, page_table, sel2, q3, q_group, kn3, vn3,
                 dec_batch, n_sel, pages_per_block):
    page, h_att, hd = cache_k.shape[2:]
    n_pages_sel = n_sel * pages_per_block
    grid_spec = pltpu.PrefetchScalarGridSpec(
        num_scalar_prefetch=2,
        grid=(dec_batch,),
        in_specs=[pl.BlockSpec((1, h_att, hd), lambda b, pt, sel: (b, q_group, 0)),
                  pl.BlockSpec((1, h_att, hd), lambda b, pt, sel: (b, 0, 0)),
                  pl.BlockSpec((1, h_att, hd), lambda b, pt, sel: (b, 0, 0)),
                  pl.BlockSpec(memory_space=pl.ANY),
                  pl.BlockSpec(memory_space=pl.ANY)],
        out_specs=pl.BlockSpec((1, h_att, hd), lambda b, pt, sel: (b, 0, 0)),
        scratch_shapes=[pltpu.VMEM((h_att, n_pages_sel, page, hd), F32),
                        pltpu.VMEM((h_att, n_pages_sel, page, hd), F32),
                        pltpu.SemaphoreType.DMA((2, h_att))],
    )
    return pl.pallas_call(
        functools.partial(_attn_sample_kernel, layer=layer, n_sel=n_sel,
                          pages_per_block=pages_per_block),
        grid_spec=grid_spec,
        out_shape=jax.ShapeDtypeStruct((dec_batch, h_att, hd), F32),
        compiler_params=_params(2 * h_att * n_pages_sel * page * hd * 4 + 16 * MIB, 1),
        name="moba_sample_attn",
    )(page_table, sel2, q3, kn3, vn3, cache_k, cache_v)


def _pad_rows(x, rows):
    return jnp.pad(x, ((0, rows - x.shape[0]), (0, 0)))


def kernel(x_prompt, x_sample, cache_k, cache_v, state_ret, page_table, w_in, w_out, norm_mix,
           ret_gn, norm_ffn, w_gate, w_up, w_down, norm_final):
    batch, seq, d_model = x_prompt.shape
    dec_batch, dec_seq, _ = x_sample.shape
    depth, n_pool, page_size, h_att, hd = cache_k.shape
    h_ret = state_ret.shape[2]
    w_ret, w_att = h_ret * hd, h_att * hd
    n_pages = page_table.shape[1]
    past_len = n_pages * page_size
    assert hd == HEAD_DIM and h_ret == h_att and dec_seq == 1
    assert MOBA_BLOCK % page_size == 0 and past_len % MOBA_BLOCK == 0
    pages_per_block = MOBA_BLOCK // page_size
    n_past_blocks = past_len // MOBA_BLOCK
    assert n_past_blocks >= MOBA_TOPK and dec_batch <= SAMPLE_ROWS
    n_groups = 7
    att_group = 4
    in_cols = w_in.shape[2]
    assert in_cols == n_groups * w_ret

    m_p = batch * seq
    yp = x_prompt.reshape(m_p, d_model)
    ys = _pad_rows(x_sample.reshape(dec_batch, d_model), SAMPLE_ROWS)
    pos_s = past_len + jnp.arange(dec_seq)

    n_mix = 5

    rp_states, ks_rows, vs_rows, rs_states = [], [], [], []
    rows5 = (depth, batch, seq, h_att, hd)
    k_prompt = v_prompt = None
    for l in range(depth):
        hp = _rmsnorm(yp, norm_mix[l], BF16, 512)
        hs = _rmsnorm(ys, norm_mix[l], BF16, SAMPLE_ROWS)
        proj_p, proj_s = _proj(hp, hs, w_in, l, n_mix * w_ret, 1024, 512)
        k_prompt, k2d, k_s = _kv_proj(hp, hs, w_in, l, 5 * w_ret, rows5, k_prompt, 512)
        v_prompt, v2d, v_s = _kv_proj(hp, hs, w_in, l, 6 * w_ret, rows5, v_prompt, 512)

        ret_o, ret_s, w_down_b = _ret_prompt(proj_p, ret_gn[l], batch, seq, h_ret, w_down, l)
        att_o, kmean = _moba_prompt(proj_p, k2d, v2d, batch, seq, h_att, att_group * h_ret,
                                    cache_k, l, page_table, pages_per_block)
        rp_states.append(ret_s)

        proj3 = proj_s.reshape(SAMPLE_ROWS, n_mix * h_ret, hd)
        kn3 = k_s.reshape(SAMPLE_ROWS, h_att, hd)
        vn3 = v_s.reshape(SAMPLE_ROWS, h_att, hd)
        ret_os, ret_ss = _ret_sample(proj3, state_ret, l, ret_gn[l], pos_s, dec_batch, h_ret)
        sel = _select_sample(kmean, proj3, att_group, MOBA_TOPK)
        att_os = _attn_sample(cache_k, cache_v, l, page_table, sel.reshape(dec_batch * 8, hd),
                              proj3, att_group, kn3, vn3, dec_batch, MOBA_TOPK, pages_per_block)
        ks_rows.append(k_s[:dec_batch].reshape(dec_batch, 1, h_att, hd))
        vs_rows.append(v_s[:dec_batch].reshape(dec_batch, 1, h_att, hd))
        rs_states.append(ret_ss[0])
        mix_r = _pad_rows(ret_os.reshape(dec_batch, w_ret), SAMPLE_ROWS).astype(BF16)
        mix_a = _pad_rows(att_os.reshape(dec_batch, w_att), SAMPLE_ROWS).astype(BF16)

        yp, ys = _outproj(ret_o, att_o, mix_r, mix_a, w_out, l, yp, ys, 1024, 512)
        hf = _rmsnorm(yp, norm_ffn[l], BF16, 512)
        hfs = _rmsnorm(ys, norm_ffn[l], BF16, SAMPLE_ROWS)
        act, acts = _gateup(hf, hfs, w_gate, w_up, l, 1024, 256)
        yp, ys = _down(act, acts, w_down_b, 0, yp, ys, 512, 512)

    y_prompt = _rmsnorm(yp, norm_final, F32, 512).reshape(batch, seq, d_model)
    y_sample = _rmsnorm(ys, norm_final, F32, SAMPLE_ROWS)[:dec_batch].reshape(dec_batch, 1, d_model)
    return (y_prompt, y_sample, k_prompt, v_prompt, jnp.stack(rp_states),
            jnp.stack(ks_rows), jnp.stack(vs_rows), jnp.stack(rs_states))
```

```python
import functools
import math

import jax
import jax.numpy as jnp
from jax import lax
from jax.experimental import pallas as pl
from jax.experimental.pallas import tpu as pltpu

HEAD_DIM = 128
RET_CHUNK = 128
MOBA_BLOCK = 256
MOBA_TOPK = 3
ROPE_BASE = 10000.0
NORM_EPS = 1e-6
GN_EPS = 1e-5

BF16 = jnp.bfloat16
F32 = jnp.float32
NEG_INF = float("-inf")
LOG2_E = math.log2(math.e)
MIB = 1024 * 1024
V7X_VMEM_BUDGET = 56 * MIB
SAMPLE_ROWS = 16
KMEAN_PAGES_PER_STEP = 16
MOBA_QROWS = 256


def _params(vmem_bytes, n_axes):
    return pltpu.CompilerParams(
        dimension_semantics=("arbitrary",) * n_axes,
        vmem_limit_bytes=int(min(max(vmem_bytes, 16 * MIB), V7X_VMEM_BUDGET)))


def _dot(a, b):
    return jnp.dot(a, b, preferred_element_type=F32)


def _dot_nt(a, b, precision=None):
    return lax.dot_general(a, b, (((1,), (1,)), ((), ())), precision=precision,
                           preferred_element_type=F32)


def _dot_tn(a, b):
    return lax.dot_general(a, b, (((0,), (0,)), ((), ())), preferred_element_type=F32)


def _silu(x):
    return x * (1.0 / (1.0 + jnp.exp(-x)))


def _rmsnorm_kernel(x_ref, g_ref, o_ref):
    x = x_ref[...]
    y = x * lax.rsqrt(jnp.mean(x * x, axis=-1, keepdims=True) + NORM_EPS)
    o_ref[...] = (y * g_ref[...]).astype(o_ref.dtype)


def _rmsnorm(x, g, out_dtype, tm):
    m, d = x.shape
    return pl.pallas_call(
        _rmsnorm_kernel,
        grid=(m // tm,),
        in_specs=[pl.BlockSpec((tm, d), lambda i: (i, 0)),
                  pl.BlockSpec((1, d), lambda i: (0, 0))],
        out_specs=pl.BlockSpec((tm, d), lambda i: (i, 0)),
        out_shape=jax.ShapeDtypeStruct((m, d), out_dtype),
        compiler_params=_params(6 * tm * d * 4, 1),
        name="rmsnorm",
    )(x, g.reshape(1, d))


def _first_m_step():
    return pl.program_id(1) == 0


def _m_tile(n, m, m_tiles):
    return jnp.where(n % 2 == 0, m, m_tiles - 1 - m)


def _a_spec(tm, width, m_tiles):
    return pl.BlockSpec((tm, width), lambda n, m: (_m_tile(n, m, m_tiles), 0))


def _o_spec(tm, tn, m_tiles):
    return pl.BlockSpec((tm, tn), lambda n, m: (_m_tile(n, m, m_tiles), n))


def _weight_spec(k, tn, layer, col_block0=0, single_buffer=False):
    mode = pl.Buffered(1) if single_buffer else None
    return pl.BlockSpec((None, k, tn), lambda n, m: (layer, 0, col_block0 + n), pipeline_mode=mode)


def _proj_kernel(a_ref, as_ref, w_ref, o_ref, os_ref, wb_ref):
    @pl.when(_first_m_step())
    def _():
        wb_ref[...] = w_ref[...].astype(BF16)
        os_ref[...] = _dot(as_ref[...], wb_ref[...])

    o_ref[...] = _dot(a_ref[...], wb_ref[...])


def _proj(a, a_s, w, layer, n_cols, tm, tn):
    m, k = a.shape
    rows_s = a_s.shape[0]
    assert m % tm == 0 and n_cols % tn == 0
    return pl.pallas_call(
        _proj_kernel,
        grid=(n_cols // tn, m // tm),
        in_specs=[_a_spec(tm, k, m // tm),
                  pl.BlockSpec((rows_s, k), lambda n, m: (0, 0)),
                  _weight_spec(k, tn, layer)],
        out_specs=[_o_spec(tm, tn, m // tm),
                   pl.BlockSpec((rows_s, tn), lambda n, m: (0, n))],
        out_shape=[jax.ShapeDtypeStruct((m, n_cols), F32),
                   jax.ShapeDtypeStruct((rows_s, n_cols), F32)],
        scratch_shapes=[pltpu.VMEM((k, tn), BF16)],
        compiler_params=_params(2 * (tm * k * 2 + k * tn * 4 + tm * tn * 4) + k * tn * 2 + 8 * MIB, 2),
        name="in_proj",
    )(a, a_s, w)


def _kv_proj_kernel(*refs, layer, heads_per_step, tiles_per_seq, n_tiles, m_tiles):
    a_ref, as_ref, w_ref = refs[:3]
    o5_ref, o2_ref, os_ref, wb_ref, stage_ref, sems = refs[-6:]
    tm = a_ref.shape[0]
    step = pl.program_id(0) * m_tiles + pl.program_id(1)

    def row_copies(at_step):
        n, m = at_step // m_tiles, at_step % m_tiles
        tile = _m_tile(n, m, m_tiles)
        b, row0 = tile // tiles_per_seq, (tile % tiles_per_seq) * tm
        slot = at_step % 2
        return [pltpu.make_async_copy(
            stage_ref.at[slot, :, pl.ds(h * HEAD_DIM, HEAD_DIM)],
            o5_ref.at[layer, b, pl.ds(row0, tm), n * heads_per_step + h, :],
            sems.at[slot]) for h in range(heads_per_step)]

    @pl.when(_first_m_step())
    def _():
        wb_ref[...] = w_ref[...].astype(BF16)
        os_ref[...] = _dot(as_ref[...], wb_ref[...])

    acc = _dot(a_ref[...], wb_ref[...])
    o2_ref[...] = acc.astype(o2_ref.dtype)

    @pl.when(step >= 2)
    def _():
        for c in row_copies(step - 2):
            c.wait()

    stage_ref[step % 2] = acc
    for c in row_copies(step):
        c.start()

    @pl.when(step == n_tiles * m_tiles - 1)
    def _():
        if n_tiles * m_tiles >= 2:
            for c in row_copies(step - 1):
                c.wait()
        for c in row_copies(step):
            c.wait()


def _kv_proj(a, a_s, w, layer, col0, rows5, prev, tm):
    m, k = a.shape
    rows_s = a_s.shape[0]
    depth, batch, seq, n_heads, hd = rows5
    heads_per_step = 8
    tn = heads_per_step * hd
    assert seq % tm == 0 and n_heads % heads_per_step == 0 and col0 % tn == 0
    tiles_per_seq = seq // tm
    m_tiles = m // tm
    in_specs = [_a_spec(tm, k, m // tm),
                pl.BlockSpec((rows_s, k), lambda n, m: (0, 0)),
                _weight_spec(k, tn, layer, col0 // tn, single_buffer=True)]
    args = [a, a_s, w]
    aliases = {}
    if prev is not None:
        in_specs.append(pl.BlockSpec(memory_space=pl.ANY))
        args.append(prev)
        aliases = {3: 0}
    n_tiles = n_heads // heads_per_step
    return pl.pallas_call(
        functools.partial(_kv_proj_kernel, layer=layer, heads_per_step=heads_per_step,
                          tiles_per_seq=tiles_per_seq, n_tiles=n_tiles, m_tiles=m_tiles),
        grid=(n_tiles, m_tiles),
        in_specs=in_specs,
        out_specs=[pl.BlockSpec(memory_space=pl.ANY),
                   _o_spec(tm, tn, m_tiles),
                   pl.BlockSpec((rows_s, tn), lambda n, m: (0, n))],
        out_shape=[jax.ShapeDtypeStruct(rows5, F32),
                   jax.ShapeDtypeStruct((m, n_heads * hd), BF16),
                   jax.ShapeDtypeStruct((rows_s, n_heads * hd), F32)],
        scratch_shapes=[pltpu.VMEM((k, tn), BF16), pltpu.VMEM((2, tm, tn), F32),
                        pltpu.SemaphoreType.DMA((2,))],
        input_output_aliases=aliases,
        compiler_params=_params(
            2 * (tm * k * 2 + tm * tn * 6) + k * tn * 6 + 8 * MIB, 2),
        name="kv_proj",
    )(*args)


def _outproj_kernel(a1_ref, a2_ref, a1s_ref, a2s_ref, w_ref, r_ref, rs_ref, o_ref, os_ref, wb_ref):
    k1 = a1_ref.shape[1]

    def mix(x1_ref, x2_ref, res_ref):
        return res_ref[...] + (_dot(x1_ref[...], wb_ref[:k1, :]) + _dot(x2_ref[...], wb_ref[k1:, :]))

    @pl.when(_first_m_step())
    def _():
        wb_ref[...] = w_ref[...].astype(BF16)
        os_ref[...] = mix(a1s_ref, a2s_ref, rs_ref)

    o_ref[...] = mix(a1_ref, a2_ref, r_ref)


def _outproj(a1, a2, a1_s, a2_s, w, layer, res, res_s, tm, tn):
    m, k1 = a1.shape
    k2 = a2.shape[1]
    rows_s = a1_s.shape[0]
    k, n = k1 + k2, w.shape[2]
    assert m % tm == 0 and n % tn == 0
    return pl.pallas_call(
        _outproj_kernel,
        grid=(n // tn, m // tm),
        in_specs=[_a_spec(tm, k1, m // tm),
                  _a_spec(tm, k2, m // tm),
                  pl.BlockSpec((rows_s, k1), lambda n, m: (0, 0)),
                  pl.BlockSpec((rows_s, k2), lambda n, m: (0, 0)),
                  _weight_spec(k, tn, layer),
                  _o_spec(tm, tn, m // tm),
                  pl.BlockSpec((rows_s, tn), lambda n, m: (0, n))],
        out_specs=[_o_spec(tm, tn, m // tm),
                   pl.BlockSpec((rows_s, tn), lambda n, m: (0, n))],
        out_shape=[jax.ShapeDtypeStruct((m, n), F32),
                   jax.ShapeDtypeStruct((rows_s, n), F32)],
        scratch_shapes=[pltpu.VMEM((k, tn), BF16)],
        compiler_params=_params(
            2 * (tm * k * 2 + k * tn * 4 + 2 * tm * tn * 4) + k * tn * 2 + 8 * MIB, 2),
        name="out_proj",
    )(a1, a2, a1_s, a2_s, w, res, res_s)


def _gateup_kernel(h_ref, hs_ref, wg_ref, wu_ref, o_ref, os_ref, wgb_ref, wub_ref):
    def swiglu_in(x_ref):
        x = x_ref[...]
        return _silu(_dot(x, wgb_ref[...])) * _dot(x, wub_ref[...])

    @pl.when(_first_m_step())
    def _():
        wgb_ref[...] = wg_ref[...].astype(BF16)
        wub_ref[...] = wu_ref[...].astype(BF16)
        os_ref[...] = swiglu_in(hs_ref).astype(os_ref.dtype)

    o_ref[...] = swiglu_in(h_ref).astype(o_ref.dtype)


def _gateup(h, h_s, wg, wu, layer, tm, tn):
    m, k = h.shape
    rows_s = h_s.shape[0]
    n = wg.shape[2]
    assert m % tm == 0 and n % tn == 0
    return pl.pallas_call(
        _gateup_kernel,
        grid=(n // tn, m // tm),
        in_specs=[_a_spec(tm, k, m // tm),
                  pl.BlockSpec((rows_s, k), lambda n, m: (0, 0)),
                  _weight_spec(k, tn, layer), _weight_spec(k, tn, layer)],
        out_specs=[_o_spec(tm, tn, m // tm),
                   pl.BlockSpec((rows_s, tn), lambda n, m: (0, n))],
        out_shape=[jax.ShapeDtypeStruct((m, n), BF16),
                   jax.ShapeDtypeStruct((rows_s, n), BF16)],
        scratch_shapes=[pltpu.VMEM((k, tn), BF16), pltpu.VMEM((k, tn), BF16)],
        compiler_params=_params(
            2 * (tm * k * 2 + 2 * k * tn * 4 + tm * tn * 2) + 2 * k * tn * 2 + 8 * MIB, 2),
        name="ffn_gate_up",
    )(h, h_s, wg, wu)


def _down_kernel(a_ref, as_ref, w_ref, r_ref, rs_ref, o_ref, os_ref):
    @pl.when(_first_m_step())
    def _():
        os_ref[...] = rs_ref[...] + _dot(as_ref[...], w_ref[...])

    o_ref[...] = r_ref[...] + _dot(a_ref[...], w_ref[...])


def _down(a, a_s, w_bf16, layer, res, res_s, tm, tn):
    m, k = a.shape
    rows_s = a_s.shape[0]
    n = w_bf16.shape[2]
    assert m % tm == 0 and n % tn == 0
    return pl.pallas_call(
        _down_kernel,
        grid=(n // tn, m // tm),
        in_specs=[_a_spec(tm, k, m // tm),
                  pl.BlockSpec((rows_s, k), lambda n, m: (0, 0)),
                  _weight_spec(k, tn, layer),
                  _o_spec(tm, tn, m // tm),
                  pl.BlockSpec((rows_s, tn), lambda n, m: (0, n))],
        out_specs=[_o_spec(tm, tn, m // tm),
                   pl.BlockSpec((rows_s, tn), lambda n, m: (0, n))],
        out_shape=[jax.ShapeDtypeStruct((m, n), F32),
                   jax.ShapeDtypeStruct((rows_s, n), F32)],
        compiler_params=_params(2 * (tm * k * 2 + k * tn * 2 + 2 * tm * tn * 4) + 8 * MIB, 2),
        name="ffn_down",
    )(a, a_s, w_bf16, res, res_s)


def _rotary_tables(pos):
    half = HEAD_DIM // 2
    inv = ROPE_BASE ** (-jnp.arange(half, dtype=F32) / half)
    ang = pos.astype(F32)[:, None] * inv[None, :]
    cos, sin = jnp.cos(ang), jnp.sin(ang)
    return (jnp.concatenate([cos, cos], axis=-1), jnp.concatenate([-sin, sin], axis=-1))


def _log_decay(h_ret):
    return jnp.log1p(-jnp.exp2(-5.0 - jnp.arange(h_ret, dtype=F32)))


def _group_norm_gate(o, gn, g):
    mu = jnp.mean(o, axis=-1, keepdims=True)
    d = o - mu
    var = jnp.mean(d * d, axis=-1, keepdims=True)
    return _silu(g) * (d * lax.rsqrt(var + GN_EPS) * gn)


def _ret_prompt_kernel(q_ref, k_ref, v_ref, g_ref, cos_ref, sin_ref, intra_ref, qdec_ref,
                       kdec_ref, cdec_ref, gn_ref, w_ref, o_ref, s_ref, wb_ref,
                       qb_ref, kb_ref, qd_ref, kd_ref, vb_ref, of_ref,
                       *, chunk, n_chunks, unroll, n_cast_steps):
    half = HEAD_DIM // 2
    scale = HEAD_DIM ** -0.5
    seq = q_ref.shape[0]
    intra = intra_ref[0]
    cdec = cdec_ref[0, 0:1, :]

    step = pl.program_id(0) * pl.num_programs(1) + pl.program_id(1)

    @pl.when(step < n_cast_steps)
    def _():
        wb_ref[...] = w_ref[...].astype(BF16)

    cos = cos_ref[...]
    sin = sin_ref[...]
    q = q_ref[...]
    q = q * cos + pltpu.roll(q, half, 1) * sin
    k = k_ref[...]
    k = (k * cos + pltpu.roll(k, half, 1) * sin) * scale
    qb_ref[...] = q.astype(BF16)
    kb_ref[...] = k.astype(BF16)
    qd = q.reshape(n_chunks, chunk, HEAD_DIM) * qdec_ref[...]
    kd = k.reshape(n_chunks, chunk, HEAD_DIM) * kdec_ref[...]
    qd_ref[...] = qd.reshape(seq, HEAD_DIM).astype(BF16)
    kd_ref[...] = kd.reshape(seq, HEAD_DIM).astype(BF16)
    vb_ref[...] = v_ref[...].astype(BF16)

    def body(c, s):
        r = pl.ds(pl.multiple_of(c * chunk, chunk), chunk)
        v = vb_ref[r, :]
        sc = _dot_nt(qb_ref[r, :], kb_ref[r, :]) * intra
        of_ref[r, :] = _dot(sc.astype(BF16), v) + _dot(qd_ref[r, :], s.astype(BF16))
        return s * cdec + _dot_tn(kd_ref[r, :], v)

    s_fin = lax.fori_loop(0, n_chunks, body, jnp.zeros((HEAD_DIM, HEAD_DIM), F32),
                          unroll=unroll)
    s_ref[0, 0] = s_fin
    o_ref[...] = _group_norm_gate(of_ref[...], gn_ref[0], g_ref[...]).astype(o_ref.dtype)


def _slab_rows(k, steps):
    for rows in range(16, k + 1, 16):
        if k % rows == 0 and k // rows <= steps:
            return rows
    raise ValueError(f"cannot split {k} rows into at most {steps} slabs")


def _ret_prompt(proj, gn_w, batch, seq, h_ret, w_stack, layer):
    w_k, w_n = w_stack.shape[1:]
    w_rows = _slab_rows(w_k, batch * h_ret)
    n_cast_steps = w_k // w_rows

    def w_index(b, h):
        return jnp.minimum(b * h_ret + h, n_cast_steps - 1), 0

    chunk = math.gcd(seq, RET_CHUNK)
    n_chunks = seq // chunk
    cos2, sin2 = _rotary_tables(jnp.arange(seq))
    lg = _log_decay(h_ret)
    i = jnp.arange(chunk, dtype=F32)
    rel = i[:, None] - i[None, :]
    intra = jnp.exp(jnp.where(rel[None] >= 0, rel[None] * lg[:, None, None], -jnp.inf))
    ones = jnp.ones((1, 1, HEAD_DIM), F32)
    qdec = jnp.exp((i + 1.0)[None, :, None] * lg[:, None, None]) * ones
    kdec = jnp.exp((chunk - 1.0 - i)[None, :, None] * lg[:, None, None]) * ones
    cdec = jnp.exp(chunk * lg)[:, None, None] * jnp.ones((1, 8, HEAD_DIM), F32)
    gn = gn_w.astype(F32).reshape(h_ret, 1, HEAD_DIM)
    hd = HEAD_DIM

    def col(group):
        return pl.BlockSpec((seq, hd), lambda b, h: (b, group * h_ret + h))

    const = lambda shape: pl.BlockSpec(shape, lambda b, h: (h, 0, 0))
    unroll = math.gcd(n_chunks, 16)
    return pl.pallas_call(
        functools.partial(_ret_prompt_kernel, chunk=chunk, n_chunks=n_chunks, unroll=unroll,
                          n_cast_steps=n_cast_steps),
        grid=(batch, h_ret),
        in_specs=[col(0), col(1), col(2), col(3),
                  pl.BlockSpec((seq, hd), lambda b, h: (0, 0)),
                  pl.BlockSpec((seq, hd), lambda b, h: (0, 0)),
                  const((1, chunk, chunk)), const((1, chunk, hd)), const((1, chunk, hd)),
                  const((1, 8, hd)), const((1, 1, hd)),
                  pl.BlockSpec((None, w_rows, w_n), lambda b, h: (layer,) + w_index(b, h))],
        out_specs=[pl.BlockSpec((seq, hd), lambda b, h: (b, h)),
                   pl.BlockSpec((1, 1, hd, hd), lambda b, h: (b, h, 0, 0)),
                   pl.BlockSpec((None, w_rows, w_n), lambda b, h: (0,) + w_index(b, h))],
        out_shape=[jax.ShapeDtypeStruct((batch * seq, h_ret * hd), BF16),
                   jax.ShapeDtypeStruct((batch, h_ret, hd, hd), F32),
                   jax.ShapeDtypeStruct((1, w_k, w_n), BF16)],
        scratch_shapes=[pltpu.VMEM((seq, hd), BF16)] * 5 + [pltpu.VMEM((seq, hd), F32)],
        compiler_params=_params(2 * 7 * seq * hd * 4 + 2 * w_rows * w_n * 6 + 16 * MIB, 2),
        name="retention_prompt",
    )(proj, proj, proj, proj, cos2, sin2, intra, qdec, kdec, cdec, gn, w_stack)


def _block_keep(gate_t, n_past, eye_bf16):
    blk_id = lax.broadcasted_iota(jnp.int32, gate_t.shape, 0)
    g = jnp.where(blk_id < n_past, gate_t, NEG_INF)
    rank = jnp.zeros(gate_t.shape, F32)
    for i in range(n_past):
        gi = g[i:i + 1, :]
        rank = rank + jnp.where(gi > g, 1.0,
                                jnp.where(gi == g, jnp.where(blk_id > i, 1.0, 0.0), 0.0))
    keep_t = jnp.where(blk_id < n_past, jnp.where(rank < float(MOBA_TOPK), 1.0, 0.0), 0.0)
    pad = jnp.zeros((HEAD_DIM - gate_t.shape[0], gate_t.shape[1]), F32)
    keep_t = jnp.concatenate([keep_t, pad], axis=0).astype(BF16)
    return _dot_tn(keep_t, eye_bf16)


def _page_block_means(pages, o_ref, pages_per_block):
    n_out = len(pages) // pages_per_block
    n_heads = pages[0].shape[3]
    inv_n = 1.0 / (pages[0].shape[2] * pages_per_block)
    for i in range(n_out):
        acc = jnp.sum(pages[i * pages_per_block][0, 0], axis=0)
        for p in range(1, pages_per_block):
            acc = acc + jnp.sum(pages[i * pages_per_block + p][0, 0], axis=0)
        acc = acc * inv_n
        for h in range(n_heads):
            o_ref[0, h, i:i + 1, :] = acc[h:h + 1, :]


def _moba_prompt_kernel(pt_ref, q_ref, k_ref, v_ref, *refs, n_blocks, pages_per_block,
                        n_kmean_steps):
    del pt_ref
    pages, (o_ref, km_ref) = refs[:-2], refs[-2:]
    blk = MOBA_BLOCK
    scale = HEAD_DIM ** -0.5

    if n_kmean_steps is None:
        _page_block_means(pages, km_ref, pages_per_block)
    else:
        step = pl.program_id(0) * pl.num_programs(1) + pl.program_id(1)

        @pl.when(step < n_kmean_steps)
        def _():
            _page_block_means(pages, km_ref, pages_per_block)

    kb = k_ref[...]
    vb = v_ref[...]
    kmean = jnp.mean(kb.astype(F32).reshape(n_blocks, blk, HEAD_DIM), axis=1)
    n_rank_rows = -(-n_blocks // 8) * 8
    kmean_pad = jnp.concatenate(
        [kmean, jnp.zeros((HEAD_DIM - n_blocks, HEAD_DIM), F32)], axis=0)
    qrows = MOBA_QROWS
    row = lax.broadcasted_iota(jnp.int32, (qrows, blk), 0)
    colv = lax.broadcasted_iota(jnp.int32, (qrows, blk), 1)
    r128 = lax.broadcasted_iota(jnp.int32, (HEAD_DIM, HEAD_DIM), 0)
    c128 = lax.broadcasted_iota(jnp.int32, (HEAD_DIM, HEAD_DIM), 1)
    eye_bf16 = jnp.where(r128 == c128, 1.0, 0.0).astype(BF16)

    def start_tile(t, u):
        r0 = t * blk + u * qrows
        qf = q_ref[r0:r0 + qrows, :]
        keep = None
        if t > MOBA_TOPK:
            gate_t = _dot_nt(kmean_pad, qf, precision=lax.Precision.HIGHEST)
            keep = _block_keep(gate_t[:n_rank_rows], t, eye_bf16)
            keep = jnp.where(keep > 0.5, 0.0, NEG_INF)
        return dict(t=t, r0=r0, causal=colv <= row + u * qrows,
                    qs=(qf * (scale * LOG2_E)).astype(BF16),
                    keep=keep, m=None, l=None, acc=None)

    def attend(tile, j):
        t = tile["t"]
        kv = slice(j * blk, (j + 1) * blk)
        s = _dot_nt(tile["qs"], kb[kv])
        if j == t:
            s = jnp.where(tile["causal"], s, NEG_INF)
        elif tile["keep"] is not None:
            s = s + tile["keep"][:, j:j + 1]
        s_max = jnp.max(s, axis=1, keepdims=True)
        if tile["m"] is None:
            m = s_max
            p = jnp.exp2(s - m)
            tile["l"] = jnp.sum(p, axis=1, keepdims=True)
            tile["acc"] = _dot(p.astype(BF16), vb[kv])
        else:
            m = jnp.maximum(tile["m"], s_max)
            alpha = jnp.exp2(tile["m"] - m)
            p = jnp.exp2(s - m)
            tile["l"] = alpha * tile["l"] + jnp.sum(p, axis=1, keepdims=True)
            tile["acc"] = alpha * tile["acc"] + _dot(p.astype(BF16), vb[kv])
        tile["m"] = m

    for t in range(n_blocks):
        for u in range(blk // qrows):
            tile = start_tile(t, u)
            for j in [t] + list(range(t)):
                attend(tile, j)
            o_ref[tile["r0"]:tile["r0"] + qrows, :] = (tile["acc"] / tile["l"]).astype(o_ref.dtype)


def _moba_prompt(proj, k2d, v2d, batch, seq, h_att, q_col0, cache, layer, page_table,
                 pages_per_block):
    assert seq % MOBA_BLOCK == 0
    n_blocks = seq // MOBA_BLOCK
    assert n_blocks <= HEAD_DIM
    hd = HEAD_DIM
    db, n_pages = page_table.shape
    page, n_heads = cache.shape[2:4]
    pps = KMEAN_PAGES_PER_STEP
    assert n_pages % pps == 0 and pps % pages_per_block == 0
    blocks_per_step = pps // pages_per_block
    assert blocks_per_step % 8 == 0
    chunks = n_pages // pps
    n_kmean_steps = db * chunks
    assert n_kmean_steps <= batch * h_att, "not enough attention steps to carry the page stream"

    def kmean_pos(b, h):
        s = jnp.minimum(b * h_att + h, n_kmean_steps - 1)
        return s // chunks, s % chunks

    def page_spec(i):
        def index(b, h, pt):
            kb, kc = kmean_pos(b, h)
            return layer, pt[kb, kc * pps + i], 0, 0, 0
        return pl.BlockSpec((1, 1, page, n_heads, hd), index)

    def kmean_index(b, h, pt):
        kb, kc = kmean_pos(b, h)
        return kb, 0, kc, 0

    head = pl.BlockSpec((seq, hd), lambda b, h, pt: (b, h))
    grid_spec = pltpu.PrefetchScalarGridSpec(
        num_scalar_prefetch=1,
        grid=(batch, h_att),
        in_specs=[pl.BlockSpec((seq, hd), lambda b, h, pt: (b, q_col0 + h)), head, head]
        + [page_spec(i) for i in range(pps)],
        out_specs=[head, pl.BlockSpec((1, n_heads, blocks_per_step, hd), kmean_index)],
    )
    return pl.pallas_call(
        functools.partial(_moba_prompt_kernel, n_blocks=n_blocks, pages_per_block=pages_per_block,
                          n_kmean_steps=None if n_kmean_steps == batch * h_att else n_kmean_steps),
        grid_spec=grid_spec,
        out_shape=[jax.ShapeDtypeStruct((batch * seq, h_att * hd), BF16),
                   jax.ShapeDtypeStruct((db, n_heads, n_pages // pages_per_block, hd), F32)],
        compiler_params=_params(2 * pps * page * n_heads * hd * 4 + 24 * MIB, 2),
        name="moba_prompt",
    )(page_table, proj, k2d, v2d, *([cache] * pps))


def _row_to_col(x_row, eye):
    n = x_row.shape[1]
    return jnp.sum(jnp.where(eye, jnp.broadcast_to(x_row, (n, n)), 0.0), axis=1, keepdims=True)


def _ret_sample_kernel(q_ref, k_ref, v_ref, g_ref, s_ref, cos_ref, sin_ref, dec_ref, gn_ref,
                       o_ref, so_ref, *, h_ret):
    half = HEAD_DIM // 2
    scale = HEAD_DIM ** -0.5
    cos = cos_ref[...]
    sin = sin_ref[...]
    q = q_ref[0]
    q = q * cos + pltpu.roll(q, half, 1) * sin
    k = k_ref[0]
    k = (k * cos + pltpu.roll(k, half, 1) * sin) * scale
    v = v_ref[0]
    r0 = lax.broadcasted_iota(jnp.int32, (HEAD_DIM, HEAD_DIM), 0)
    r1 = lax.broadcasted_iota(jnp.int32, (HEAD_DIM, HEAD_DIM), 1)
    eye = r0 == r1
    for h in range(h_ret):
        hh = slice(h, h + 1)
        kcol = _row_to_col(k[hh, :], eye)
        qcol = _row_to_col(q[hh, :], eye)
        s_new = s_ref[0, 0, h] * dec_ref[hh, :] + kcol * v[hh, :]
        so_ref[0, 0, h] = s_new
        o = jnp.sum(qcol * s_new, axis=0, keepdims=True)
        o_ref[0, hh, :] = _group_norm_gate(o, gn_ref[hh, :], g_ref[0, hh, :]).astype(o_ref.dtype)


def _ret_sample(proj3, state, layer, gn_w, pos, dec_batch, h_ret):
    hd = HEAD_DIM
    cos2, sin2 = _rotary_tables(pos)
    dec = jnp.exp(1.0 * _log_decay(h_ret))[:, None] * jnp.ones((1, hd), F32)
    gn = gn_w.astype(F32).reshape(h_ret, hd)
    grp = lambda g: pl.BlockSpec((1, h_ret, hd), lambda b: (b, g, 0))
    full = lambda shape: pl.BlockSpec(shape, lambda b: (0,) * len(shape))
    return pl.pallas_call(
        functools.partial(_ret_sample_kernel, h_ret=h_ret),
        grid=(dec_batch,),
        in_specs=[grp(0), grp(1), grp(2), grp(3),
                  pl.BlockSpec((1, 1, h_ret, hd, hd), lambda b: (layer, b, 0, 0, 0)),
                  full((1, hd)), full((1, hd)), full((h_ret, hd)), full((h_ret, hd))],
        out_specs=[pl.BlockSpec((1, h_ret, hd), lambda b: (b, 0, 0)),
                   pl.BlockSpec((1, 1, h_ret, hd, hd), lambda b: (0, b, 0, 0, 0))],
        out_shape=[jax.ShapeDtypeStruct((dec_batch, h_ret, hd), F32),
                   jax.ShapeDtypeStruct((1, dec_batch, h_ret, hd, hd), F32)],
        compiler_params=_params(16 * MIB, 1),
        name="retention_sample",
    )(proj3, proj3, proj3, proj3, state, cos2, sin2, dec, gn)


def _select_kernel(km_ref, q_ref, o_ref, *, n_sel):
    n_heads, nb = km_ref.shape[1], km_ref.shape[2]
    blk_id = lax.broadcasted_iota(jnp.int32, (nb, 1), 0).astype(F32)
    row = lax.broadcasted_iota(jnp.int32, o_ref.shape[1:], 0)
    lane = lax.broadcasted_iota(jnp.int32, o_ref.shape[1:], 1)
    out = jnp.zeros(o_ref.shape[1:], jnp.int32)
    for h in range(n_heads):
        gate = jnp.sum(km_ref[0, h] * q_ref[0, h:h + 1, :], axis=1, keepdims=True)
        for r in range(n_sel):
            best = jnp.max(gate, axis=0, keepdims=True)
            idx = jnp.min(jnp.where(gate == best, blk_id, float(nb)), axis=0, keepdims=True)
            out = jnp.where((row == r) & (lane == h), idx.astype(jnp.int32), out)
            gate = jnp.where(blk_id == idx, NEG_INF, gate)
    o_ref[0] = out


def _select_sample(kmean, proj3, q_group, n_sel):
    db, n_heads, nb, hd = kmean.shape
    assert n_sel <= 8 and n_heads <= hd
    return pl.pallas_call(
        functools.partial(_select_kernel, n_sel=n_sel),
        grid=(db,),
        in_specs=[pl.BlockSpec((1, n_heads, nb, hd), lambda b: (b, 0, 0, 0)),
                  pl.BlockSpec((1, n_heads, hd), lambda b: (b, q_group, 0))],
        out_specs=pl.BlockSpec((1, 8, hd), lambda b: (b, 0, 0)),
        out_shape=jax.ShapeDtypeStruct((db, 8, hd), jnp.int32),
        compiler_params=_params(16 * MIB, 1),
        name="moba_sample_select",
    )(kmean, proj3)


def _attn_sample_kernel(pt_ref, sel_ref, q_ref, kn_ref, vn_ref, ck_ref, cv_ref, o_ref,
                        kbuf, vbuf, sems, *, layer, n_sel, pages_per_block):
    b = pl.program_id(0)
    n_heads, n_pages_sel, page, hd = kbuf.shape
    scale = hd ** -0.5

    def head_copies(h):
        copies = []
        for r in range(n_sel):
            blk = sel_ref[b * 8 + r, h]
            for i in range(pages_per_block):
                phys = pt_ref[b, blk * pages_per_block + i]
                slot = r * pages_per_block + i
                copies.append(pltpu.make_async_copy(
                    ck_ref.at[layer, phys, :, h, :], kbuf.at[h, slot], sems.at[0, h]))
                copies.append(pltpu.make_async_copy(
                    cv_ref.at[layer, phys, :, h, :], vbuf.at[h, slot], sems.at[1, h]))
        return copies

    for h in range(n_heads):
        for c in head_copies(h):
            c.start()

    for h in range(n_heads):
        for c in head_copies(h):
            c.wait()
        hh = slice(h, h + 1)
        q = q_ref[0, hh, :]
        k_all = kbuf[h].reshape(n_pages_sel * page, hd).astype(BF16)
        v_all = vbuf[h].reshape(n_pages_sel * page, hd).astype(BF16)
        q8 = jnp.broadcast_to(q * scale, (8, hd)).astype(BF16)
        s = _dot_nt(q8, k_all)
        s_own = jnp.sum(kn_ref[0, hh, :] * q, axis=1, keepdims=True) * scale
        m = jnp.maximum(jnp.max(s, axis=1, keepdims=True), s_own)
        e = jnp.exp(s - m)
        e_own = jnp.exp(s_own - m)
        l = jnp.sum(e, axis=1, keepdims=True) + e_own
        o = _dot(e.astype(BF16), v_all) + e_own * vn_ref[0, hh, :]
        o_ref[0, hh, :] = (o / l)[0:1, :]


def _attn_sample(cache_k, cache_v, layer, page_table, sel2, q3, q_group, kn3, vn3,
                 dec_batch, n_sel, pages_per_block):
    page, h_att, hd = cache_k.shape[2:]
    n_pages_sel = n_sel * pages_per_block
    grid_spec = pltpu.PrefetchScalarGridSpec(
        num_scalar_prefetch=2,
        grid=(dec_batch,),
        in_specs=[pl.BlockSpec((1, h_att, hd), lambda b, pt, sel: (b, q_group, 0)),
                  pl.BlockSpec((1, h_att, hd), lambda b, pt, sel: (b, 0, 0)),
                  pl.BlockSpec((1, h_att, hd), lambda b, pt, sel: (b, 0, 0)),
                  pl.BlockSpec(memory_space=pl.ANY),
                  pl.BlockSpec(memory_space=pl.ANY)],
        out_specs=pl.BlockSpec((1, h_att, hd), lambda b, pt, sel: (b, 0, 0)),
        scratch_shapes=[pltpu.VMEM((h_att, n_pages_sel, page, hd), F32),
                        pltpu.VMEM((h_att, n_pages_sel, page, hd), F32),
                        pltpu.SemaphoreType.DMA((2, h_att))],
    )
    return pl.pallas_call(
        functools.partial(_attn_sample_kernel, layer=layer, n_sel=n_sel,
                          pages_per_block=pages_per_block),
        grid_spec=grid_spec,
        out_shape=jax.ShapeDtypeStruct((dec_batch, h_att, hd), F32),
        compiler_params=_params(2 * h_att * n_pages_sel * page * hd * 4 + 16 * MIB, 1),
        name="moba_sample_attn",
    )(page_table, sel2, q3, kn3, vn3, cache_k, cache_v)


def _pad_rows(x, rows):
    return jnp.pad(x, ((0, rows - x.shape[0]), (0, 0)))


def kernel(x_prompt, x_sample, cache_k, cache_v, state_ret, page_table, w_in, w_out, norm_mix,
           ret_gn, norm_ffn, w_gate, w_up, w_down, norm_final):
    batch, seq, d_model = x_prompt.shape
    dec_batch, dec_seq, _ = x_sample.shape
    depth, n_pool, page_size, h_att, hd = cache_k.shape
    h_ret = state_ret.shape[2]
    w_ret, w_att = h_ret * hd, h_att * hd
    n_pages = page_table.shape[1]
    past_len = n_pages * page_size
    assert hd == HEAD_DIM and h_ret == h_att and dec_seq == 1
    assert MOBA_BLOCK % page_size == 0 and past_len % MOBA_BLOCK == 0
    pages_per_block = MOBA_BLOCK // page_size
    n_past_blocks = past_len // MOBA_BLOCK
    assert n_past_blocks >= MOBA_TOPK and dec_batch <= SAMPLE_ROWS
    n_groups = 7
    att_group = 4
    in_cols = w_in.shape[2]
    assert in_cols == n_groups * w_ret

    m_p = batch * seq
    yp = x_prompt.reshape(m_p, d_model)
    ys = _pad_rows(x_sample.reshape(dec_batch, d_model), SAMPLE_ROWS)
    pos_s = past_len + jnp.arange(dec_seq)

    n_mix = 5

    rp_states, ks_rows, vs_rows, rs_states = [], [], [], []
    rows5 = (depth, batch, seq, h_att, hd)
    k_prompt = v_prompt = None
    for l in range(depth):
        hp = _rmsnorm(yp, norm_mix[l], BF16, 512)
        hs = _rmsnorm(ys, norm_mix[l], BF16, SAMPLE_ROWS)
        proj_p, proj_s = _proj(hp, hs, w_in, l, n_mix * w_ret, 1024, 512)
        k_prompt, k2d, k_s = _kv_proj(hp, hs, w_in, l, 5 * w_ret, rows5, k_prompt, 512)
        v_prompt, v2d, v_s = _kv_proj(hp, hs, w_in, l, 6 * w_ret, rows5, v_prompt, 512)

        ret_o, ret_s, w_down_b = _ret_prompt(proj_p, ret_gn[l], batch, seq, h_ret, w_down, l)
        att_o, kmean = _moba_prompt(proj_p, k2d, v2d, batch, seq, h_att, att_group * h_ret,
                                    cache_k, l, page_table, pages_per_block)
        rp_states.append(ret_s)

        proj3 = proj_s.reshape(SAMPLE_ROWS, n_mix * h_ret, hd)
        kn3 = k_s.reshape(SAMPLE_ROWS, h_att, hd)
        vn3 = v_s.reshape(SAMPLE_ROWS, h_att, hd)
        ret_os, ret_ss = _ret_sample(proj3, state_ret, l, ret_gn[l], pos_s, dec_batch, h_ret)
        sel = _select_sample(kmean, proj3, att_group, MOBA_TOPK)
        att_os = _attn_sample(cache_k, cache_v, l, page_table, sel.reshape(dec_batch * 8, hd),
                              proj3, att_group, kn3, vn3, dec_batch, MOBA_TOPK, pages_per_block)
        ks_rows.append(k_s[:dec_batch].reshape(dec_batch, 1, h_att, hd))
        vs_rows.append(v_s[:dec_batch].reshape(dec_batch, 1, h_att, hd))
        rs_states.append(ret_ss[0])
        mix_r = _pad_rows(ret_os.reshape(dec_batch, w_ret), SAMPLE_ROWS).astype(BF16)
        mix_a = _pad_rows(att_os.reshape(dec_batch, w_att), SAMPLE_ROWS).astype(BF16)

        yp, ys = _outproj(ret_o, att_o, mix_r, mix_a, w_out, l, yp, ys, 1024, 512)
        hf = _rmsnorm(yp, norm_ffn[l], BF16, 512)
        hfs = _rmsnorm(ys, norm_ffn[l], BF16, SAMPLE_ROWS)
        act, acts = _gateup(hf, hfs, w_gate, w_up, l, 1024, 256)
        yp, ys = _down(act, acts, w_down_b, 0, yp, ys, 512, 512)

    y_prompt = _rmsnorm(yp, norm_final, F32, 512).reshape(batch, seq, d_model)
    y_sample = _rmsnorm(ys, norm_final, F32, SAMPLE_ROWS)[:dec_batch].reshape(dec_batch, 1, d_model)
    return (y_prompt, y_sample, k_prompt, v_prompt, jnp.stack(rp_states),
            jnp.stack(ks_rows), jnp.stack(vs_rows), jnp.stack(rs_states))
```
